```python
import jax, jax.numpy as jnp
from jax import lax
import numpy as np

D_MODEL = 2048
BATCH = 8
SEQ = 4096
DEPTH = 2

GRID_W = 64
CTX_LEN = 256
N_MOD = 6
MLP_CHUNK = 128
MLP_INNER = D_MODEL
MLP_GROUPS = 16
MLP_GROUP_DIM = MLP_INNER // MLP_GROUPS
NA_HEAD_DIM = 128
NA_HEADS = D_MODEL // NA_HEAD_DIM
NA_ROWS = 8
NA_COLS = 16
NA_QB = 16
FFN_DIM = 5632
MOE_EXPERTS = 8
MOE_TOP_K = 2
MOE_DIM = 7168
MOE_BLOCK = 256
N_EVEN = (DEPTH + 1) // 2
N_ODD = DEPTH // 2
EPS = 1e-6
NEG_INF = -1e30

kernel_name = "hybrid_gmlp_natten_moe_dit"


def rms_norm(t, w):
    tf = t.astype(jnp.float32)
    y = tf * lax.rsqrt(jnp.mean(tf * tf, axis=-1, keepdims=True) + EPS)
    return (y * w.astype(jnp.float32)).astype(t.dtype)


def _modulate(t, w, shift, scale):
    return rms_norm(t, w) * (1 + scale) + shift


def swiglu(t, w1, w3, w2):
    return (jax.nn.silu(t @ w1) * (t @ w3)) @ w2


def chunk_mlp(h, w_in, g_v, w_s, b_s, w_out):
    b, l, _ = h.shape
    z = jax.nn.gelu(h @ w_in, approximate=False)
    u, v = jnp.split(z, 2, axis=-1)
    v = rms_norm(v, g_v).reshape(b, l // MLP_CHUNK, MLP_CHUNK, MLP_GROUPS, MLP_GROUP_DIM)
    v = jnp.einsum('gpq,bnqgc->bnpgc', w_s, v) + b_s.T[:, :, None]
    return (u * v.reshape(b, l, MLP_INNER)) @ w_out


def _column_blocks():
    ncb = GRID_W // NA_QB
    kwb = min(NA_QB + NA_COLS, GRID_W)
    qcol = np.arange(GRID_W).reshape(ncb, NA_QB)
    cs = np.clip(qcol - NA_COLS // 2, 0, GRID_W - NA_COLS)
    kc0 = np.clip(np.arange(ncb) * NA_QB - NA_COLS // 2, 0, GRID_W - kwb)
    colblk = kc0[:, None] + np.arange(kwb)
    kcol = colblk[:, None, :]
    inwin = (kcol >= cs[..., None]) & (kcol < cs[..., None] + NA_COLS)
    dc = np.clip(kcol - qcol[:, :, None] + NA_COLS - 1, 0, 2 * NA_COLS - 2)
    return colblk, dc, inwin, ncb, kwb


def neighbourhood_attention(h, hc, w_qkv, g_q, g_k, rpb, w_o, ctx_out):
    b, l, _ = h.shape
    n_ctx = hc.shape[1]
    rows = l // GRID_W
    kh = min(NA_ROWS, rows)
    scale = NA_HEAD_DIM ** -0.5

    def proj(t):
        qkv = (t @ w_qkv).reshape(t.shape[0], t.shape[1], 3, NA_HEADS, NA_HEAD_DIM)
        return rms_norm(qkv[:, :, 0], g_q) * scale, rms_norm(qkv[:, :, 1], g_k), qkv[:, :, 2]

    q, k, v = proj(h)
    qc, kc, vc = proj(hc)
    colblk, dc, inwin, ncb, kwb = _column_blocks()
    qg = q.reshape(b, rows, GRID_W, NA_HEADS, NA_HEAD_DIM).transpose(1, 0, 2, 3, 4)
    kg = k.reshape(b, rows, GRID_W, NA_HEADS, NA_HEAD_DIM)
    vg = v.reshape(b, rows, GRID_W, NA_HEADS, NA_HEAD_DIM)
    rpb32 = rpb.astype(jnp.float32)
    n_loc = kh * kwb

    def row_attend(args):
        q_r, r = args
        rs = jnp.clip(r - kh // 2, 0, rows - kh)
        kb = lax.dynamic_slice_in_dim(kg, rs, kh, axis=1)[:, :, colblk]
        vb = lax.dynamic_slice_in_dim(vg, rs, kh, axis=1)[:, :, colblk]
        qb = q_r.reshape(b, ncb, NA_QB, NA_HEADS, NA_HEAD_DIM)
        s_loc = jnp.einsum('bnqhd,binjhd->bnhqij', qb, kb).astype(jnp.float32)
        dr = rs + jnp.arange(kh) - r + (NA_ROWS - 1)
        bias = rpb32[:, dr][:, :, dc].transpose(2, 0, 3, 1, 4)
        s_loc = jnp.where(inwin[:, None, :, None, :], s_loc + bias, NEG_INF)
        s_ctx = jnp.einsum('bnqhd,bkhd->bnhqk', qb, kc).astype(jnp.float32)
        s = jnp.concatenate([s_loc.reshape(b, ncb, NA_HEADS, NA_QB, n_loc), s_ctx], axis=-1)
        p = jax.nn.softmax(s, axis=-1).astype(v.dtype)
        p_loc = p[..., :n_loc].reshape(b, ncb, NA_HEADS, NA_QB, kh, kwb)
        o = (jnp.einsum('bnhqij,binjhd->bnqhd', p_loc, vb)
             + jnp.einsum('bnhqk,bkhd->bnqhd', p[..., n_loc:], vc))
        return o.reshape(b, GRID_W, NA_HEADS, NA_HEAD_DIM)

    o = lax.map(row_attend, (qg, jnp.arange(rows)))
    y = o.transpose(1, 0, 2, 3, 4).reshape(b, l, D_MODEL) @ w_o
    yc = None
    if ctx_out:
        sc = jnp.einsum('bqhd,bkhd->bhqk', qc, kc).astype(jnp.float32)
        pc = jax.nn.softmax(sc, axis=-1).astype(vc.dtype)
        yc = jnp.einsum('bhqk,bkhd->bqhd', pc, vc).reshape(b, n_ctx, D_MODEL) @ w_o
    return y, yc


def moe_swiglu(t, w_router, w1, w3, w2):
    n_tok, d = t.shape
    logits = (t @ w_router).astype(jnp.float32)
    top_v, top_i = lax.top_k(logits, MOE_TOP_K)
    gates = jax.nn.softmax(top_v, axis=-1)
    n_asg = n_tok * MOE_TOP_K
    e_flat = top_i.reshape(-1)
    order = jnp.argsort(e_flat, stable=True)
    e_sorted = e_flat[order]
    tok_sorted = (order // MOE_TOP_K).astype(jnp.int32)
    gate_sorted = gates.reshape(-1)[order]
    counts = jnp.bincount(e_flat, length=MOE_EXPERTS)
    padded = (counts + MOE_BLOCK - 1) // MOE_BLOCK * MOE_BLOCK
    pad_end = jnp.cumsum(padded)
    pad_start = pad_end - padded
    grp_start = jnp.cumsum(counts) - counts
    slot = pad_start[e_sorted] + jnp.arange(n_asg) - grp_start[e_sorted]
    n_blocks = -(-n_asg // MOE_BLOCK) + MOE_EXPERTS
    slot_tok = jnp.zeros((n_blocks * MOE_BLOCK,), jnp.int32).at[slot].set(tok_sorted)
    blk_e = jnp.minimum(jnp.searchsorted(pad_end, jnp.arange(n_blocks) * MOE_BLOCK, side='right'),
                        MOE_EXPERTS - 1)
    xs = t[slot_tok].reshape(n_blocks, MOE_BLOCK, d)

    def expert_block(args):
        xb, e = args
        return swiglu(xb, w1[e], w3[e], w2[e])

    ys = lax.map(expert_block, (xs, blk_e)).reshape(n_blocks * MOE_BLOCK, d)
    return jnp.zeros_like(t).at[tok_sorted].add(ys[slot] * gate_sorted[:, None].astype(t.dtype))


def setup_inputs(seed: int = 0) -> dict:
    key = jax.random.key(seed)
    ks = jax.random.split(key, 26)
    f32 = jnp.float32

    def nrm(k, shape, s):
        return jax.random.normal(k, shape, f32) * s

    d = D_MODEL
    return {
        "x": nrm(ks[0], (BATCH, SEQ, d), 1.0),
        "c": nrm(ks[1], (BATCH, d), 1.0),
        "ctx": nrm(ks[2], (BATCH, CTX_LEN, d), 1.0),
        "c_ctx": nrm(ks[3], (d,), 1.0),
        "w_ada": nrm(ks[4], (DEPTH, d, N_MOD * d), 0.5 * d ** -0.5),
        "b_ada": nrm(ks[5], (DEPTH, N_MOD * d), 0.02),
        "norm_w": 1.0 + nrm(ks[6], (DEPTH, 2, d), 0.02),
        "mlp_w_in": nrm(ks[7], (N_EVEN, d, 2 * MLP_INNER), d ** -0.5),
        "mlp_g_v": 1.0 + nrm(ks[8], (N_EVEN, MLP_INNER), 0.02),
        "mlp_w_s": nrm(ks[9], (N_EVEN, MLP_GROUPS, MLP_CHUNK, MLP_CHUNK), MLP_CHUNK ** -0.5),
        "mlp_b_s": 1.0 + nrm(ks[10], (N_EVEN, MLP_GROUPS, MLP_CHUNK), 0.02),
        "mlp_w_out": nrm(ks[11], (N_EVEN, MLP_INNER, d), MLP_INNER ** -0.5),
        "ffn_w1": nrm(ks[12], (N_EVEN, d, FFN_DIM), d ** -0.5),
        "ffn_w3": nrm(ks[13], (N_EVEN, d, FFN_DIM), d ** -0.5),
        "ffn_w2": nrm(ks[14], (N_EVEN, FFN_DIM, d), FFN_DIM ** -0.5),
        "na_w_qkv": nrm(ks[15], (N_ODD, d, 3 * d), d ** -0.5),
        "na_g_q": 1.0 + nrm(ks[16], (N_ODD, NA_HEAD_DIM), 0.02),
        "na_g_k": 1.0 + nrm(ks[17], (N_ODD, NA_HEAD_DIM), 0.02),
        "na_rpb": nrm(ks[18], (N_ODD, NA_HEADS, 2 * NA_ROWS - 1, 2 * NA_COLS - 1), 0.1),
        "na_w_o": nrm(ks[19], (N_ODD, d, d), d ** -0.5),
        "moe_w_router": nrm(ks[20], (N_ODD, d, MOE_EXPERTS), d ** -0.5),
        "moe_w1": nrm(ks[21], (N_ODD, MOE_EXPERTS, d, MOE_DIM), d ** -0.5),
        "moe_w3": nrm(ks[22], (N_ODD, MOE_EXPERTS, d, MOE_DIM), d ** -0.5),
        "moe_w2": nrm(ks[23], (N_ODD, MOE_EXPERTS, MOE_DIM, d), MOE_DIM ** -0.5),
    }


def reference(x, c, ctx, c_ctx, w_ada, b_ada, norm_w, mlp_w_in, mlp_g_v, mlp_w_s, mlp_b_s,
              mlp_w_out, ffn_w1, ffn_w3, ffn_w2, na_w_qkv, na_g_q, na_g_k, na_rpb, na_w_o,
              moe_w_router, moe_w1, moe_w3, moe_w2):
    b, l, d = x.shape
    xc = ctx
    for i in range(DEPTH):
        j = i // 2
        ctx_out = i != DEPTH - 1
        odd = i % 2 == 1
        mod = jax.nn.silu(c) @ w_ada[i] + b_ada[i]
        sh1, sc1, g1, sh2, sc2, g2 = jnp.split(mod[:, None, :], N_MOD, axis=-1)
        if ctx_out or odd:
            mod_c = jax.nn.silu(c_ctx) @ w_ada[i] + b_ada[i]
            csh1, csc1, cg1, csh2, csc2, cg2 = jnp.split(mod_c, N_MOD, axis=-1)
            hc = _modulate(xc, norm_w[i, 0], csh1, csc1)
        h = _modulate(x, norm_w[i, 0], sh1, sc1)
        if not odd:
            y = chunk_mlp(h, mlp_w_in[j], mlp_g_v[j], mlp_w_s[j], mlp_b_s[j], mlp_w_out[j])
            yc = chunk_mlp(hc, mlp_w_in[j], mlp_g_v[j], mlp_w_s[j], mlp_b_s[j], mlp_w_out[j]) if ctx_out else None
        else:
            y, yc = neighbourhood_attention(h, hc, na_w_qkv[j], na_g_q[j], na_g_k[j], na_rpb[j],
                                            na_w_o[j], ctx_out)
        x = x + g1 * y
        toks = _modulate(x, norm_w[i, 1], sh2, sc2).reshape(b * l, d)
        if ctx_out:
            xc = xc + cg1 * yc
            toks = jnp.concatenate(
                [toks, _modulate(xc, norm_w[i, 1], csh2, csc2).reshape(-1, d)], axis=0)
        if not odd:
            f = swiglu(toks, ffn_w1[j], ffn_w3[j], ffn_w2[j])
        else:
            f = moe_swiglu(toks, moe_w_router[j], moe_w1[j], moe_w3[j], moe_w2[j])
        x = x + g2 * f[:b * l].reshape(b, l, d)
        if ctx_out:
            xc = xc + cg2 * f[b * l:].reshape(xc.shape)
    return x
```

```python
import functools

import numpy as np
import jax
import jax.numpy as jnp
from jax import lax
from jax.experimental import pallas as pl
from jax.experimental.pallas import tpu as pltpu

GRID_W = 64
N_MOD = 6
MLP_CHUNK = 128
MLP_GROUPS = 16
NA_HEAD_DIM = 128
NA_ROWS = 8
NA_COLS = 16
MOE_EXPERTS = 8
MOE_TOP_K = 2
EPS = 1e-6
NEG_INF = -1e30

LANE = 128
MOD_ROWS = 16
VMEM_LIMIT = 56 * 1024 * 1024

F32 = jnp.float32
BF16 = jnp.bfloat16


def _dot(a, b):
    return jnp.dot(a, b, preferred_element_type=F32)


def _dot_t(a, b):
    return lax.dot_general(a, b, (((1,), (1,)), ((), ())), preferred_element_type=F32)


def _split_bf16(a):
    hi = a.astype(BF16)
    lo = (a - hi.astype(F32)).astype(BF16)
    return hi, lo


def _silu(a):
    return a * jax.nn.sigmoid(a)


def _norm_mod(x, nw, shift, scale):
    ms = jnp.mean(x * x, axis=-1, keepdims=True)
    y = x * lax.rsqrt(ms + EPS)
    return (y * nw) * (1.0 + scale) + shift


def _mod_slices(mod_ref, row, d, first):
    base = 0 if first else 3
    return tuple(mod_ref[pl.ds(row, 1), (base + k) * d:(base + k + 1) * d] for k in range(3))


def _params(sem):
    return pltpu.CompilerParams(dimension_semantics=sem, vmem_limit_bytes=VMEM_LIMIT)


def _ada_kernel(a_ref, w_ref, b_ref, o_ref):
    a_hi, a_lo = _split_bf16(_silu(a_ref[...]))
    w_hi, w_lo = _split_bf16(w_ref[...])
    o_ref[...] = _dot(a_hi, w_hi) + _dot(a_lo, w_hi) + _dot(a_hi, w_lo) + b_ref[...]


def _ada_mod(cc, w_ada, b_ada, tn=1024):
    nl, d, n = w_ada.shape
    tn = min(tn, d)
    return pl.pallas_call(
        _ada_kernel,
        grid=(nl, n // tn),
        in_specs=[
            pl.BlockSpec((MOD_ROWS, d), lambda l, j: (0, 0)),
            pl.BlockSpec((None, d, tn), lambda l, j: (l, 0, j)),
            pl.BlockSpec((None, 1, tn), lambda l, j: (l, 0, j)),
        ],
        out_specs=pl.BlockSpec((None, MOD_ROWS, tn), lambda l, j: (l, 0, j)),
        out_shape=jax.ShapeDtypeStruct((nl, MOD_ROWS, n), F32),
        compiler_params=_params(("arbitrary", "arbitrary")),
        name="ada_mod",
    )(cc, w_ada, b_ada.reshape(nl, 1, n))


def _mixer_kernel(x_ref, mod_ref, nw_ref, win_ref, gv_ref, ws_ref, bs_ref, wout_ref, o_ref,
                  u_s, v_s, m_s, *, mod_row, col_chunk):
    tm, d = x_ref.shape
    inner = u_s.shape[1]
    gd = inner // MLP_GROUPS
    row = pl.program_id(0) if mod_row is None else mod_row
    sh, sc, gate = _mod_slices(mod_ref, row, d, True)
    x = x_ref[...]
    h = _norm_mod(x, nw_ref[...], sh, sc).astype(BF16)
    ss = jnp.zeros((tm, 1), F32)
    for c in range(2 * inner // col_chunk):
        lo = c * col_chunk
        z = _dot(h, win_ref[:, lo:lo + col_chunk])
        z = 0.5 * z * (1.0 + lax.erf(z * (2.0 ** -0.5)))
        if lo < inner:
            u_s[:, lo:lo + col_chunk] = z
        else:
            v_s[:, lo - inner:lo - inner + col_chunk] = z
            ss = ss + jnp.sum(z * z, axis=-1, keepdims=True)
    rinv = lax.rsqrt(ss / inner + EPS)
    for n in range(tm // MLP_CHUNK):
        rows = slice(n * MLP_CHUNK, (n + 1) * MLP_CHUNK)
        for g in range(MLP_GROUPS):
            cols = slice(g * gd, (g + 1) * gd)
            vb = ((v_s[rows, cols] * rinv[rows]) * gv_ref[:, cols]).astype(BF16)
            mixed = _dot(ws_ref[g], vb) + bs_ref[:, cols]
            m_s[rows, cols] = (u_s[rows, cols] * mixed).astype(BF16)
    y = _dot(m_s[...], wout_ref[...])
    o_ref[...] = x + gate * y


def _mixer(x, mod, nw, w_in, g_v, w_s, b_full, w_out, *, mod_row, tm):
    nb, s, d = x.shape
    inner = w_out.shape[0]
    tm = min(tm, s)
    const = lambda *shape: pl.BlockSpec(shape, lambda b, i: (0,) * len(shape))
    return pl.pallas_call(
        functools.partial(_mixer_kernel, mod_row=mod_row, col_chunk=min(512, inner)),
        grid=(nb, s // tm),
        in_specs=[
            pl.BlockSpec((None, tm, d), lambda b, i: (b, i, 0)),
            const(*mod.shape), const(1, d), const(d, 2 * inner), const(1, inner),
            const(*w_s.shape), const(MLP_CHUNK, inner), const(inner, d),
        ],
        out_specs=pl.BlockSpec((None, tm, d), lambda b, i: (b, i, 0)),
        out_shape=jax.ShapeDtypeStruct(x.shape, F32),
        scratch_shapes=[pltpu.VMEM((tm, inner), F32), pltpu.VMEM((tm, inner), F32),
                        pltpu.VMEM((tm, inner), BF16)],
        compiler_params=_params(("arbitrary", "arbitrary")),
        name="gmlp_mixer",
    )(x, mod, nw, w_in, g_v, w_s, b_full, w_out)


def _ffn_kernel(x_ref, mod_ref, nw_ref, w1_ref, w3_ref, w2_ref, o_ref, t_s, acc_s, *, mod_row):
    d = x_ref.shape[-1]
    f = pl.program_id(2)
    row = pl.program_id(0) if mod_row is None else mod_row
    sh, sc, gate = _mod_slices(mod_ref, row, d, False)

    @pl.when(f == 0)
    def _():
        t_s[...] = _norm_mod(x_ref[...], nw_ref[...], sh, sc).astype(BF16)
        acc_s[...] = jnp.zeros_like(acc_s)

    t = t_s[...]
    hmid = (_silu(_dot(t, w1_ref[...])) * _dot(t, w3_ref[...])).astype(BF16)
    acc_s[...] += _dot(hmid, w2_ref[...])

    @pl.when(f == pl.num_programs(2) - 1)
    def _():
        o_ref[...] = x_ref[...] + gate * acc_s[...]


def _ffn(x, mod, nw, w1, w3, w2, *, mod_row, tm, tf):
    nb, s, d = x.shape
    tm = min(tm, s)
    nf = w1.shape[1] // tf
    return pl.pallas_call(
        functools.partial(_ffn_kernel, mod_row=mod_row),
        grid=(nb, s // tm, nf),
        in_specs=[
            pl.BlockSpec((None, tm, d), lambda b, i, f: (b, i, 0)),
            pl.BlockSpec(mod.shape, lambda b, i, f: (0, 0)),
            pl.BlockSpec((1, d), lambda b, i, f: (0, 0)),
            pl.BlockSpec((d, tf), lambda b, i, f: (0, f)),
            pl.BlockSpec((d, tf), lambda b, i, f: (0, f)),
            pl.BlockSpec((tf, d), lambda b, i, f: (f, 0)),
        ],
        out_specs=pl.BlockSpec((None, tm, d), lambda b, i, f: (b, i, 0)),
        out_shape=jax.ShapeDtypeStruct(x.shape, F32),
        scratch_shapes=[pltpu.VMEM((tm, d), BF16), pltpu.VMEM((tm, d), F32)],
        compiler_params=_params(("arbitrary", "arbitrary", "arbitrary")),
        name="ffn_swiglu",
    )(x, mod, nw, w1, w3, w2)


def _qkv_kernel(x_ref, mod_ref, nw_ref, w_ref, gq_ref, gk_ref, o_ref, h_s, *, mod_row, j0):
    d = x_ref.shape[-1]
    jj = pl.program_id(2)
    j = jj + j0
    row = pl.program_id(0) if mod_row is None else mod_row
    sh, sc, _ = _mod_slices(mod_ref, row, d, True)

    @pl.when(jj == 0)
    def _():
        h_s[...] = _norm_mod(x_ref[...], nw_ref[...], sh, sc).astype(BF16)

    y = _dot(h_s[...], w_ref[...])

    @pl.when(j == 2)
    def _():
        o_ref[...] = y.astype(BF16)

    @pl.when(j < 2)
    def _():
        is_q = j == 0
        g = jnp.where(is_q, gq_ref[...], gk_ref[...])
        post = jnp.where(is_q, NA_HEAD_DIM ** -0.5, 1.0).astype(F32)
        for hh in range(d // NA_HEAD_DIM):
            cols = slice(hh * NA_HEAD_DIM, (hh + 1) * NA_HEAD_DIM)
            blk = y[:, cols]
            ms = jnp.mean(blk * blk, axis=-1, keepdims=True)
            o_ref[:, cols] = (((blk * lax.rsqrt(ms + EPS)) * g) * post).astype(BF16)


def _qkv(x, mod, nw, w_qkv, g_q, g_k, *, mod_row, tm, j0):
    nb, s, d = x.shape
    tm = min(tm, s)
    nj = 3 - j0
    return pl.pallas_call(
        functools.partial(_qkv_kernel, mod_row=mod_row, j0=j0),
        grid=(nb, s // tm, nj),
        in_specs=[
            pl.BlockSpec((None, tm, d), lambda b, i, j: (b, i, 0)),
            pl.BlockSpec(mod.shape, lambda b, i, j: (0, 0)),
            pl.BlockSpec((1, d), lambda b, i, j: (0, 0)),
            pl.BlockSpec((d, d), lambda b, i, j: (0, j + j0)),
            pl.BlockSpec((1, NA_HEAD_DIM), lambda b, i, j: (0, 0)),
            pl.BlockSpec((1, NA_HEAD_DIM), lambda b, i, j: (0, 0)),
        ],
        out_specs=pl.BlockSpec((None, None, tm, d), lambda b, i, j: (j, b, i, 0)),
        out_shape=jax.ShapeDtypeStruct((nj, nb, s, d), BF16),
        scratch_shapes=[pltpu.VMEM((tm, d), BF16)],
        compiler_params=_params(("arbitrary", "arbitrary", "arbitrary")),
        name="qkv_proj",
    )(x, mod, nw, w_qkv, g_q, g_k)


def _bias_table(rpb):
    qcol = np.arange(GRID_W)[:, None]
    kcol = np.arange(GRID_W)[None, :]
    cs = np.clip(qcol - NA_COLS // 2, 0, GRID_W - NA_COLS)
    inwin = (kcol >= cs) & (kcol < cs + NA_COLS)
    dc = np.clip(kcol - qcol + NA_COLS - 1, 0, 2 * NA_COLS - 2)
    tab = jnp.where(inwin[None, None], rpb.astype(F32)[:, :, dc], NEG_INF)
    return jnp.concatenate([tab[:, :-1], tab[:, 1:]], axis=-1)


def _attn_kernel(q_ref, k_ref, v_ref, kc_ref, vc_ref, tab_ref, o_ref, *, kh):
    d = q_ref.shape[-1]
    rows = k_ref.shape[0] // GRID_W
    r = pl.program_id(1)
    rs = jnp.clip(r - kh // 2, 0, rows - kh)
    dr0 = rs - r + (NA_ROWS - 1)
    start = pl.multiple_of(rs * GRID_W, GRID_W)
    win = pl.ds(start, kh * GRID_W)
    for hh in range(d // NA_HEAD_DIM):
        cols = slice(hh * NA_HEAD_DIM, (hh + 1) * NA_HEAD_DIM)
        qh = q_ref[:, cols]
        bias = jnp.concatenate([tab_ref[hh, pl.ds(dr0 + 2 * j, 1)][0] for j in range(kh // 2)], axis=-1)
        s = _dot_t(qh, k_ref[win, cols]) + bias
        sc = _dot_t(qh, kc_ref[:, cols])
        m = jnp.maximum(jnp.max(s, axis=-1, keepdims=True), jnp.max(sc, axis=-1, keepdims=True))
        p = jnp.exp(s - m)
        pc = jnp.exp(sc - m)
        den = jnp.sum(p, axis=-1, keepdims=True) + jnp.sum(pc, axis=-1, keepdims=True)
        o = _dot(p.astype(BF16), v_ref[win, cols]) + _dot(pc.astype(BF16), vc_ref[:, cols])
        o_ref[:, cols] = (o / den).astype(BF16)


def _attention(qkv, kvc, tab):
    _, nb, s, d = qkv.shape
    n_ctx = kvc.shape[2] // nb
    rows = s // GRID_W
    kh = min(NA_ROWS, rows)
    once = pl.Buffered(1)
    return pl.pallas_call(
        functools.partial(_attn_kernel, kh=kh),
        grid=(nb, rows),
        in_specs=[
            pl.BlockSpec((None, None, GRID_W, d), lambda b, r: (0, b, r, 0)),
            pl.BlockSpec((None, None, s, d), lambda b, r: (1, b, 0, 0), pipeline_mode=once),
            pl.BlockSpec((None, None, s, d), lambda b, r: (2, b, 0, 0), pipeline_mode=once),
            pl.BlockSpec((None, None, n_ctx, d), lambda b, r: (0, 0, b, 0)),
            pl.BlockSpec((None, None, n_ctx, d), lambda b, r: (1, 0, b, 0)),
            pl.BlockSpec(tab.shape, lambda b, r: (0, 0, 0, 0), pipeline_mode=once),
        ],
        out_specs=pl.BlockSpec((None, GRID_W, d), lambda b, r: (b, r, 0)),
        out_shape=jax.ShapeDtypeStruct((nb, s, d), BF16),
        compiler_params=_params(("arbitrary", "arbitrary")),
        name="nbr_attention",
    )(qkv, qkv, qkv, kvc, kvc, tab)


def _proj_route_kernel(o_ref, x_ref, mod_ref, nw_ref, wo_ref, wrh_ref, wrl_ref, x2_ref, tok_ref, route_ref):
    tm, d = x_ref.shape
    row = pl.program_id(0)
    _, _, gate1 = _mod_slices(mod_ref, row, d, True)
    sh, sc, _ = _mod_slices(mod_ref, row, d, False)
    x2 = x_ref[...] + gate1 * _dot(o_ref[...], wo_ref[...])
    x2_ref[...] = x2
    t = _norm_mod(x2, nw_ref[...], sh, sc)
    tok_ref[...] = t
    t_hi, t_lo = _split_bf16(t)
    logits = _dot(t_hi, wrh_ref[...]) + _dot(t_lo, wrh_ref[...]) + _dot(t_hi, wrl_ref[...])
    lane = lax.broadcasted_iota(jnp.int32, (tm, LANE), 1)
    logits = jnp.where(lane < MOE_EXPERTS, logits, -jnp.inf)
    m1 = jnp.max(logits, axis=-1, keepdims=True)
    i1 = jnp.min(jnp.where(logits == m1, lane, LANE), axis=-1, keepdims=True)
    rest = jnp.where(lane == i1, -jnp.inf, logits)
    m2 = jnp.max(rest, axis=-1, keepdims=True)
    i2 = jnp.min(jnp.where(rest == m2, lane, LANE), axis=-1, keepdims=True)
    e = jnp.exp(m2 - m1)
    den = 1.0 + e
    route = jnp.where(lane == 0, i1.astype(F32),
                      jnp.where(lane == 1, i2.astype(F32),
                                jnp.where(lane == 2, 1.0 / den, jnp.where(lane == 3, e / den, 0.0))))
    route_ref[...] = route


def _proj_route(o, x, mod, nw, w_o, wr_hi, wr_lo, *, tm):
    nb, s, d = x.shape
    const = lambda *shape: pl.BlockSpec(shape, lambda b, i: (0,) * len(shape))
    tile = lambda w: pl.BlockSpec((None, tm, w), lambda b, i: (b, i, 0))
    return pl.pallas_call(
        _proj_route_kernel,
        grid=(nb, s // tm),
        in_specs=[tile(d), tile(d), const(*mod.shape), const(1, d), const(d, d), const(d, LANE), const(d, LANE)],
        out_specs=[tile(d), tile(d), tile(LANE)],
        out_shape=[jax.ShapeDtypeStruct(x.shape, F32), jax.ShapeDtypeStruct(x.shape, F32),
                   jax.ShapeDtypeStruct((nb, s, LANE), F32)],
        compiler_params=_params(("arbitrary", "arbitrary")),
        name="attn_proj_route",
    )(o, x, mod, nw, w_o, wr_hi, wr_lo)


def _dispatch_kernel(slot_ref, tok_hbm, xs_in_hbm, xs_hbm, sem, *, rows):
    del xs_in_hbm
    base = pl.program_id(0) * rows

    def row_copy(t, s):
        return pltpu.make_async_copy(tok_hbm.at[pl.ds(t, 1)], xs_hbm.at[pl.ds(s, 1)], sem)

    def start(r, carry):
        for k in range(MOE_TOP_K):
            row_copy(base + r, slot_ref[0, 0, MOE_TOP_K * r + k]).start()
        return carry

    def wait(r, carry):
        for k in range(MOE_TOP_K):
            row_copy(0, 0).wait()
        return carry

    lax.fori_loop(0, rows, start, 0)
    lax.fori_loop(0, rows, wait, 0)


def _dispatch(slot, toks, n_slots, *, rows):
    n_tok, d = toks.shape
    steps = n_tok // rows
    return pl.pallas_call(
        functools.partial(_dispatch_kernel, rows=rows),
        grid=(steps,),
        in_specs=[
            pl.BlockSpec((1, 1, MOE_TOP_K * rows), lambda i: (i, 0, 0), memory_space=pltpu.SMEM),
            pl.BlockSpec(memory_space=pl.ANY),
            pl.BlockSpec(memory_space=pl.ANY),
        ],
        out_specs=pl.BlockSpec(memory_space=pl.ANY),
        out_shape=jax.ShapeDtypeStruct((n_slots, d), toks.dtype),
        scratch_shapes=[pltpu.SemaphoreType.DMA],
        input_output_aliases={2: 0},
        compiler_params=_params(("arbitrary",)),
        name="moe_dispatch",
    )(slot.reshape(steps, 1, MOE_TOP_K * rows), toks, jnp.zeros((n_slots, d), toks.dtype))


def _expert_kernel(be_ref, nu_ref, x_ref, w1_ref, w3_ref, w2_ref, y_ref, xb_s, acc_s):
    del be_ref
    b = pl.program_id(0)
    f = pl.program_id(1)
    last = f == pl.num_programs(1) - 1
    used = b < nu_ref[0]

    @pl.when(used)
    def _():
        @pl.when(f == 0)
        def _():
            xb_s[...] = x_ref[...].astype(BF16)
            acc_s[...] = jnp.zeros_like(acc_s)

        xb = xb_s[...]
        hmid = (_silu(_dot(xb, w1_ref[...])) * _dot(xb, w3_ref[...])).astype(BF16)
        acc_s[...] += _dot(hmid, w2_ref[...])

        @pl.when(last)
        def _():
            y_ref[...] = acc_s[...]

    @pl.when(jnp.logical_and(jnp.logical_not(used), last))
    def _():
        y_ref[...] = jnp.zeros_like(y_ref)


def _experts(blk_e, n_used, xs, w1, w3, w2, *, bm, tf):
    n_slots, d = xs.shape
    nf = w1.shape[2] // tf

    def wcol(b, f, be, nu):
        return (be[b], 0, jnp.where(b < nu[0], f, nf - 1))

    def wrow(b, f, be, nu):
        return (be[b], jnp.where(b < nu[0], f, nf - 1), 0)

    return pl.pallas_call(
        _expert_kernel,
        grid_spec=pltpu.PrefetchScalarGridSpec(
            num_scalar_prefetch=2,
            grid=(n_slots // bm, nf),
            in_specs=[
                pl.BlockSpec((bm, d), lambda b, f, be, nu: (b, 0)),
                pl.BlockSpec((None, d, tf), wcol),
                pl.BlockSpec((None, d, tf), wcol),
                pl.BlockSpec((None, tf, d), wrow),
            ],
            out_specs=pl.BlockSpec((bm, d), lambda b, f, be, nu: (b, 0)),
            scratch_shapes=[pltpu.VMEM((bm, d), BF16), pltpu.VMEM((bm, d), F32)],
        ),
        out_shape=jax.ShapeDtypeStruct((n_slots, d), F32),
        compiler_params=_params(("arbitrary", "arbitrary")),
        name="moe_experts",
    )(blk_e, n_used, xs, w1, w3, w2)


def _combine_kernel(slot_ref, route_ref, x_ref, mod_ref, y_hbm, o_ref, y0_s, y1_s, sem):
    rows, d = x_ref.shape
    bufs = (y0_s, y1_s)

    def row_copy(s, k, r):
        return pltpu.make_async_copy(y_hbm.at[pl.ds(s, 1)], bufs[k].at[pl.ds(r, 1)], sem)

    def start(r, carry):
        for k in range(MOE_TOP_K):
            row_copy(slot_ref[0, 0, MOE_TOP_K * r + k], k, r).start()
        return carry

    def wait(r, carry):
        for k in range(MOE_TOP_K):
            row_copy(0, k, 0).wait()
        return carry

    lax.fori_loop(0, rows, start, 0)
    lax.fori_loop(0, rows, wait, 0)
    _, _, gate2 = _mod_slices(mod_ref, pl.program_id(0), d, False)
    route = route_ref[...]
    f = route[:, 2:3] * y0_s[...] + route[:, 3:4] * y1_s[...]
    o_ref[...] = x_ref[...] + gate2 * f


def _combine(slot, route, x, mod, y, *, rows):
    nb, s, d = x.shape
    steps = s // rows
    return pl.pallas_call(
        _combine_kernel,
        grid=(nb, steps),
        in_specs=[
            pl.BlockSpec((1, 1, MOE_TOP_K * rows), lambda b, i: (b * steps + i, 0, 0), memory_space=pltpu.SMEM),
            pl.BlockSpec((None, rows, LANE), lambda b, i: (b, i, 0)),
            pl.BlockSpec((None, rows, d), lambda b, i: (b, i, 0)),
            pl.BlockSpec(mod.shape, lambda b, i: (0, 0)),
            pl.BlockSpec(memory_space=pl.ANY),
        ],
        out_specs=pl.BlockSpec((None, rows, d), lambda b, i: (b, i, 0)),
        out_shape=jax.ShapeDtypeStruct(x.shape, F32),
        scratch_shapes=[pltpu.VMEM((rows, d), F32), pltpu.VMEM((rows, d), F32), pltpu.SemaphoreType.DMA],
        compiler_params=_params(("arbitrary", "arbitrary")),
        name="moe_combine",
    )(slot.reshape(nb * steps, 1, MOE_TOP_K * rows), route, x, mod, y)


def _moe_plan(route, bm):
    n_tok = route.shape[0]
    e_flat = route[:, :MOE_TOP_K].astype(jnp.int32).reshape(-1)
    onehot = (e_flat[:, None] == jnp.arange(MOE_EXPERTS)[None, :]).astype(jnp.int32)
    csum = jnp.cumsum(onehot, axis=0)
    counts = csum[-1]
    rank = jnp.sum((csum - onehot) * onehot, axis=1)
    padded = (counts + bm - 1) // bm * bm
    pad_end = jnp.cumsum(padded)
    slot = (pad_end - padded)[e_flat] + rank
    n_blocks = (n_tok * MOE_TOP_K) // bm + MOE_EXPERTS
    blk_e = jnp.minimum(jnp.searchsorted(pad_end, jnp.arange(n_blocks) * bm, side="right"), MOE_EXPERTS - 1)
    return slot.astype(jnp.int32), blk_e.astype(jnp.int32), (pad_end[-1:] // bm).astype(jnp.int32), n_blocks


def kernel(x, c, ctx, c_ctx, w_ada, b_ada, norm_w, mlp_w_in, mlp_g_v, mlp_w_s, mlp_b_s, mlp_w_out,
           ffn_w1, ffn_w3, ffn_w2, na_w_qkv, na_g_q, na_g_k, na_rpb, na_w_o,
           moe_w_router, moe_w1, moe_w3, moe_w2):
    nb, s, d = x.shape
    n_ctx = ctx.shape[1]
    assert nb + 1 <= MOD_ROWS and w_ada.shape[0] == 2
    inner = mlp_w_out.shape[1]
    assert inner // MLP_GROUPS == LANE and d % NA_HEAD_DIM == 0

    cc = jnp.zeros((MOD_ROWS, d), F32).at[:nb].set(c).at[nb].set(c_ctx)
    mod = _ada_mod(cc, w_ada, b_ada)
    ctx_flat = ctx.reshape(1, nb * n_ctx, d)

    nw0 = norm_w[0]
    w_in = mlp_w_in[0].astype(BF16)
    w_s = mlp_w_s[0].astype(BF16)
    w_out = mlp_w_out[0].astype(BF16)
    g_v = mlp_g_v[0][None]
    b_full = jnp.repeat(mlp_b_s[0].T, inner // MLP_GROUPS, axis=1)
    mix = functools.partial(_mixer, mod=mod[0], nw=nw0[0][None], w_in=w_in, g_v=g_v, w_s=w_s, b_full=b_full,
                            w_out=w_out, tm=256)
    x1 = mix(x, mod_row=None)
    xc1 = mix(ctx_flat, mod_row=nb)
    ffn = functools.partial(_ffn, mod=mod[0], nw=nw0[1][None], w1=ffn_w1[0].astype(BF16),
                            w3=ffn_w3[0].astype(BF16), w2=ffn_w2[0].astype(BF16), tm=512, tf=512)
    x1 = ffn(x1, mod_row=None)
    xc1 = ffn(xc1, mod_row=nb)

    nw1 = norm_w[1]
    w_qkv = na_w_qkv[0].astype(BF16)
    qkv_fn = functools.partial(_qkv, mod=mod[1], nw=nw1[0][None], w_qkv=w_qkv, g_q=na_g_q[0][None],
                               g_k=na_g_k[0][None], tm=512)
    qkv = qkv_fn(x1, mod_row=None, j0=0)
    kvc = qkv_fn(xc1, mod_row=nb, j0=1)
    o = _attention(qkv, kvc, _bias_table(na_rpb[0]))

    wr = jnp.zeros((d, LANE), F32).at[:, :MOE_EXPERTS].set(moe_w_router[0])
    wr_hi, wr_lo = _split_bf16(wr)
    x2, toks, route = _proj_route(o, x1, mod[1], nw1[1][None], na_w_o[0].astype(BF16), wr_hi, wr_lo, tm=512)

    bm = 512
    n_tok = nb * s
    route_flat = route.reshape(n_tok, LANE)
    slot, blk_e, n_used, n_blocks = _moe_plan(route_flat, bm)
    xs = _dispatch(slot, toks.reshape(n_tok, d), n_blocks * bm, rows=512)
    ys = _experts(blk_e, n_used, xs, moe_w1[0].astype(BF16), moe_w3[0].astype(BF16), moe_w2[0].astype(BF16),
                  bm=bm, tf=512)
    return _combine(slot, route, x2, mod[1], ys, rows=256)
```

```python
import functools

import numpy as np
import jax
import jax.numpy as jnp
from jax import lax
from jax.experimental import pallas as pl
from jax.experimental.pallas import tpu as pltpu

GRID_W = 64
N_MOD = 6
MLP_CHUNK = 128
MLP_GROUPS = 16
NA_HEAD_DIM = 128
NA_ROWS = 8
NA_COLS = 16
MOE_EXPERTS = 8
MOE_TOP_K = 2
EPS = 1e-6
NEG_INF = -1e30

LANE = 128
MOD_ROWS = 16
VMEM_LIMIT = 56 * 1024 * 1024

F32 = jnp.float32
BF16 = jnp.bfloat16


def _dot(a, b):
    return jnp.dot(a, b, preferred_element_type=F32)


def _dot_t(a, b):
    return lax.dot_general(a, b, (((1,), (1,)), ((), ())), preferred_element_type=F32)


def _split_bf16(a):
    hi = a.astype(BF16)
    lo = (a - hi.astype(F32)).astype(BF16)
    return hi, lo


def _silu(a):
    return a * jax.nn.sigmoid(a)


def _norm_mod(x, nw, shift, scale):
    ms = jnp.mean(x * x, axis=-1, keepdims=True)
    y = x * lax.rsqrt(ms + EPS)
    return (y * nw) * (1.0 + scale) + shift


def _mod_slices(mod_ref, row, d, first):
    base = 0 if first else 3
    return tuple(mod_ref[pl.ds(row, 1), (base + k) * d:(base + k + 1) * d] for k in range(3))


def _params(sem):
    return pltpu.CompilerParams(dimension_semantics=sem, vmem_limit_bytes=VMEM_LIMIT)


def _ada_kernel(a_ref, w_ref, b_ref, o_ref):
    a_hi, a_lo = _split_bf16(_silu(a_ref[...]))
    w_hi, w_lo = _split_bf16(w_ref[...])
    o_ref[...] = _dot(a_hi, w_hi) + _dot(a_lo, w_hi) + _dot(a_hi, w_lo) + b_ref[...]


def _ada_mod(cc, w_ada, b_ada, tn=1024):
    nl, d, n = w_ada.shape
    tn = min(tn, d)
    return pl.pallas_call(
        _ada_kernel,
        grid=(nl, n // tn),
        in_specs=[
            pl.BlockSpec((MOD_ROWS, d), lambda l, j: (0, 0)),
            pl.BlockSpec((None, d, tn), lambda l, j: (l, 0, j)),
            pl.BlockSpec((None, 1, tn), lambda l, j: (l, 0, j)),
        ],
        out_specs=pl.BlockSpec((None, MOD_ROWS, tn), lambda l, j: (l, 0, j)),
        out_shape=jax.ShapeDtypeStruct((nl, MOD_ROWS, n), F32),
        compiler_params=_params(("arbitrary", "arbitrary")),
        name="ada_mod",
    )(cc, w_ada, b_ada.reshape(nl, 1, n))


def _mixer_kernel(x_ref, mod_ref, nw_ref, win_ref, gv_ref, ws_ref, bs_ref, wout_ref, o_ref,
                  u_s, v_s, m_s, *, mod_row, col_chunk):
    tm, d = x_ref.shape
    inner = u_s.shape[1]
    gd = inner // MLP_GROUPS
    row = pl.program_id(0) if mod_row is None else mod_row
    sh, sc, gate = _mod_slices(mod_ref, row, d, True)
    x = x_ref[...]
    h = _norm_mod(x, nw_ref[...], sh, sc).astype(BF16)
    ss = jnp.zeros((tm, 1), F32)
    for c in range(2 * inner // col_chunk):
        lo = c * col_chunk
        z = _dot(h, win_ref[:, lo:lo + col_chunk])
        z = 0.5 * z * (1.0 + lax.erf(z * (2.0 ** -0.5)))
        if lo < inner:
            u_s[:, lo:lo + col_chunk] = z
        else:
            v_s[:, lo - inner:lo - inner + col_chunk] = z
            ss = ss + jnp.sum(z * z, axis=-1, keepdims=True)
    rinv = lax.rsqrt(ss / inner + EPS)
    for n in range(tm // MLP_CHUNK):
        rows = slice(n * MLP_CHUNK, (n + 1) * MLP_CHUNK)
        for g in range(MLP_GROUPS):
            cols = slice(g * gd, (g + 1) * gd)
            vb = ((v_s[rows, cols] * rinv[rows]) * gv_ref[:, cols]).astype(BF16)
            mixed = _dot(ws_ref[g], vb) + bs_ref[:, cols]
            m_s[rows, cols] = (u_s[rows, cols] * mixed).astype(BF16)
    y = _dot(m_s[...], wout_ref[...])
    o_ref[...] = x + gate * y


def _mixer(x, mod, nw, w_in, g_v, w_s, b_full, w_out, *, mod_row, tm):
    nb, s, d = x.shape
    inner = w_out.shape[0]
    tm = min(tm, s)
    const = lambda *shape: pl.BlockSpec(shape, lambda b, i: (0,) * len(shape))
    return pl.pallas_call(
        functools.partial(_mixer_kernel, mod_row=mod_row, col_chunk=min(512, inner)),
        grid=(nb, s // tm),
        in_specs=[
            pl.BlockSpec((None, tm, d), lambda b, i: (b, i, 0)),
            const(*mod.shape), const(1, d), const(d, 2 * inner), const(1, inner),
            const(*w_s.shape), const(MLP_CHUNK, inner), const(inner, d),
        ],
        out_specs=pl.BlockSpec((None, tm, d), lambda b, i: (b, i, 0)),
        out_shape=jax.ShapeDtypeStruct(x.shape, F32),
        scratch_shapes=[pltpu.VMEM((tm, inner), F32), pltpu.VMEM((tm, inner), F32),
                        pltpu.VMEM((tm, inner), BF16)],
        compiler_params=_params(("arbitrary", "arbitrary")),
        name="gmlp_mixer",
    )(x, mod, nw, w_in, g_v, w_s, b_full, w_out)


def _ffn_kernel(x_ref, mod_ref, nw_ref, w1_ref, w3_ref, w2_ref, o_ref, t_s, acc_s, *, mod_row):
    d = x_ref.shape[-1]
    f = pl.program_id(2)
    row = pl.program_id(0) if mod_row is None else mod_row
    sh, sc, gate = _mod_slices(mod_ref, row, d, False)

    @pl.when(f == 0)
    def _():
        t_s[...] = _norm_mod(x_ref[...], nw_ref[...], sh, sc).astype(BF16)
        acc_s[...] = jnp.zeros_like(acc_s)

    t = t_s[...]
    hmid = (_silu(_dot(t, w1_ref[...])) * _dot(t, w3_ref[...])).astype(BF16)
    acc_s[...] += _dot(hmid, w2_ref[...])

    @pl.when(f == pl.num_programs(2) - 1)
    def _():
        o_ref[...] = x_ref[...] + gate * acc_s[...]


def _ffn(x, mod, nw, w1, w3, w2, *, mod_row, tm, tf):
    nb, s, d = x.shape
    tm = min(tm, s)
    nf = w1.shape[1] // tf
    return pl.pallas_call(
        functools.partial(_ffn_kernel, mod_row=mod_row),
        grid=(nb, s // tm, nf),
        in_specs=[
            pl.BlockSpec((None, tm, d), lambda b, i, f: (b, i, 0)),
            pl.BlockSpec(mod.shape, lambda b, i, f: (0, 0)),
            pl.BlockSpec((1, d), lambda b, i, f: (0, 0)),
            pl.BlockSpec((d, tf), lambda b, i, f: (0, f)),
            pl.BlockSpec((d, tf), lambda b, i, f: (0, f)),
            pl.BlockSpec((tf, d), lambda b, i, f: (f, 0)),
        ],
        out_specs=pl.BlockSpec((None, tm, d), lambda b, i, f: (b, i, 0)),
        out_shape=jax.ShapeDtypeStruct(x.shape, F32),
        scratch_shapes=[pltpu.VMEM((tm, d), BF16), pltpu.VMEM((tm, d), F32)],
        compiler_params=_params(("arbitrary", "arbitrary", "arbitrary")),
        name="ffn_swiglu",
    )(x, mod, nw, w1, w3, w2)


def _qkv_kernel(x_ref, mod_ref, nw_ref, w_ref, gq_ref, gk_ref, o_ref, h_s, *, mod_row, j0):
    d = x_ref.shape[-1]
    jj = pl.program_id(2)
    j = jj + j0
    row = pl.program_id(0) if mod_row is None else mod_row
    sh, sc, _ = _mod_slices(mod_ref, row, d, True)

    @pl.when(jj == 0)
    def _():
        h_s[...] = _norm_mod(x_ref[...], nw_ref[...], sh, sc).astype(BF16)

    y = _dot(h_s[...], w_ref[...])

    @pl.when(j == 2)
    def _():
        o_ref[...] = y.astype(BF16)

    @pl.when(j < 2)
    def _():
        is_q = j == 0
        g = jnp.where(is_q, gq_ref[...], gk_ref[...])
        post = jnp.where(is_q, NA_HEAD_DIM ** -0.5, 1.0).astype(F32)
        for hh in range(d // NA_HEAD_DIM):
            cols = slice(hh * NA_HEAD_DIM, (hh + 1) * NA_HEAD_DIM)
            blk = y[:, cols]
            ms = jnp.mean(blk * blk, axis=-1, keepdims=True)
            o_ref[:, cols] = (((blk * lax.rsqrt(ms + EPS)) * g) * post).astype(BF16)


def _qkv(x, mod, nw, w_qkv, g_q, g_k, *, mod_row, tm, j0):
    nb, s, d = x.shape
    tm = min(tm, s)
    nj = 3 - j0
    return pl.pallas_call(
        functools.partial(_qkv_kernel, mod_row=mod_row, j0=j0),
        grid=(nb, s // tm, nj),
        in_specs=[
            pl.BlockSpec((None, tm, d), lambda b, i, j: (b, i, 0)),
            pl.BlockSpec(mod.shape, lambda b, i, j: (0, 0)),
            pl.BlockSpec((1, d), lambda b, i, j: (0, 0)),
            pl.BlockSpec((d, d), lambda b, i, j: (0, j + j0)),
            pl.BlockSpec((1, NA_HEAD_DIM), lambda b, i, j: (0, 0)),
            pl.BlockSpec((1, NA_HEAD_DIM), lambda b, i, j: (0, 0)),
        ],
        out_specs=pl.BlockSpec((None, None, tm, d), lambda b, i, j: (j, b, i, 0)),
        out_shape=jax.ShapeDtypeStruct((nj, nb, s, d), BF16),
        scratch_shapes=[pltpu.VMEM((tm, d), BF16)],
        compiler_params=_params(("arbitrary", "arbitrary", "arbitrary")),
        name="qkv_proj",
    )(x, mod, nw, w_qkv, g_q, g_k)


def _bias_table(rpb):
    qcol = np.arange(GRID_W)[:, None]
    kcol = np.arange(GRID_W)[None, :]
    cs = np.clip(qcol - NA_COLS // 2, 0, GRID_W - NA_COLS)
    inwin = (kcol >= cs) & (kcol < cs + NA_COLS)
    dc = np.clip(kcol - qcol + NA_COLS - 1, 0, 2 * NA_COLS - 2)
    tab = jnp.where(inwin[None, None], rpb.astype(F32)[:, :, dc], NEG_INF)
    return jnp.concatenate([tab[:, :-1], tab[:, 1:]], axis=-1)


def _attn_kernel(q_ref, k_ref, v_ref, kc_ref, vc_ref, tab_ref, o_ref, s_s, p_s, *, kh):
    d = q_ref.shape[-1]
    heads = d // NA_HEAD_DIM
    rows = k_ref.shape[0] // GRID_W
    n_loc = kh * GRID_W
    r = pl.program_id(1)
    rs = jnp.clip(r - kh // 2, 0, rows - kh)
    dr0 = rs - r + (NA_ROWS - 1)
    start = pl.multiple_of(rs * GRID_W, GRID_W)
    win = pl.ds(start, n_loc)
    head_cols = [slice(hh * NA_HEAD_DIM, (hh + 1) * NA_HEAD_DIM) for hh in range(heads)]
    for hh, cols in enumerate(head_cols):
        qh = q_ref[:, cols]
        bias = jnp.concatenate([tab_ref[hh, pl.ds(dr0 + 2 * j, 1)][0] for j in range(kh // 2)], axis=-1)
        s_s[hh, :, :n_loc] = _dot_t(qh, k_ref[win, cols]) + bias
        s_s[hh, :, n_loc:] = _dot_t(qh, kc_ref[:, cols])
    dens = []
    for hh in range(heads):
        s = s_s[hh]
        p = jnp.exp(s - jnp.max(s, axis=-1, keepdims=True))
        dens.append(jnp.sum(p, axis=-1, keepdims=True))
        p_s[hh] = p.astype(BF16)
    for hh, cols in enumerate(head_cols):
        o = _dot(p_s[hh, :, :n_loc], v_ref[win, cols]) + _dot(p_s[hh, :, n_loc:], vc_ref[:, cols])
        o_ref[:, cols] = (o / dens[hh]).astype(BF16)


def _attention(qkv, kvc, tab):
    _, nb, s, d = qkv.shape
    n_ctx = kvc.shape[2] // nb
    rows = s // GRID_W
    kh = min(NA_ROWS, rows)
    once = pl.Buffered(1)
    return pl.pallas_call(
        functools.partial(_attn_kernel, kh=kh),
        grid=(nb, rows),
        in_specs=[
            pl.BlockSpec((None, None, GRID_W, d), lambda b, r: (0, b, r, 0)),
            pl.BlockSpec((None, None, s, d), lambda b, r: (1, b, 0, 0), pipeline_mode=once),
            pl.BlockSpec((None, None, s, d), lambda b, r: (2, b, 0, 0), pipeline_mode=once),
            pl.BlockSpec((None, None, n_ctx, d), lambda b, r: (0, 0, b, 0)),
            pl.BlockSpec((None, None, n_ctx, d), lambda b, r: (1, 0, b, 0)),
            pl.BlockSpec(tab.shape, lambda b, r: (0, 0, 0, 0), pipeline_mode=once),
        ],
        out_specs=pl.BlockSpec((None, GRID_W, d), lambda b, r: (b, r, 0)),
        out_shape=jax.ShapeDtypeStruct((nb, s, d), BF16),
        scratch_shapes=[pltpu.VMEM((d // NA_HEAD_DIM, GRID_W, kh * GRID_W + n_ctx), F32),
                        pltpu.VMEM((d // NA_HEAD_DIM, GRID_W, kh * GRID_W + n_ctx), BF16)],
        compiler_params=_params(("arbitrary", "arbitrary")),
        name="nbr_attention",
    )(qkv, qkv, qkv, kvc, kvc, tab)


def _proj_route_kernel(o_ref, x_ref, mod_ref, nw_ref, wo_ref, wrh_ref, wrl_ref, x2_ref, tok_ref, route_ref):
    tm, d = x_ref.shape
    row = pl.program_id(0)
    _, _, gate1 = _mod_slices(mod_ref, row, d, True)
    sh, sc, _ = _mod_slices(mod_ref, row, d, False)
    x2 = x_ref[...] + gate1 * _dot(o_ref[...], wo_ref[...])
    x2_ref[...] = x2
    t = _norm_mod(x2, nw_ref[...], sh, sc)
    tok_ref[...] = t
    t_hi, t_lo = _split_bf16(t)
    logits = _dot(t_hi, wrh_ref[...]) + _dot(t_lo, wrh_ref[...]) + _dot(t_hi, wrl_ref[...])
    lane = lax.broadcasted_iota(jnp.int32, (tm, LANE), 1)
    logits = jnp.where(lane < MOE_EXPERTS, logits, -jnp.inf)
    m1 = jnp.max(logits, axis=-1, keepdims=True)
    i1 = jnp.min(jnp.where(logits == m1, lane, LANE), axis=-1, keepdims=True)
    rest = jnp.where(lane == i1, -jnp.inf, logits)
    m2 = jnp.max(rest, axis=-1, keepdims=True)
    i2 = jnp.min(jnp.where(rest == m2, lane, LANE), axis=-1, keepdims=True)
    e = jnp.exp(m2 - m1)
    den = 1.0 + e
    route = jnp.where(lane == 0, i1.astype(F32),
                      jnp.where(lane == 1, i2.astype(F32),
                                jnp.where(lane == 2, 1.0 / den, jnp.where(lane == 3, e / den, 0.0))))
    route_ref[...] = route


def _proj_route(o, x, mod, nw, w_o, wr_hi, wr_lo, *, tm):
    nb, s, d = x.shape
    const = lambda *shape: pl.BlockSpec(shape, lambda b, i: (0,) * len(shape))
    tile = lambda w: pl.BlockSpec((None, tm, w), lambda b, i: (b, i, 0))
    return pl.pallas_call(
        _proj_route_kernel,
        grid=(nb, s // tm),
        in_specs=[tile(d), tile(d), const(*mod.shape), const(1, d), const(d, d), const(d, LANE), const(d, LANE)],
        out_specs=[tile(d), tile(d), tile(LANE)],
        out_shape=[jax.ShapeDtypeStruct(x.shape, F32), jax.ShapeDtypeStruct(x.shape, F32),
                   jax.ShapeDtypeStruct((nb, s, LANE), F32)],
        compiler_params=_params(("arbitrary", "arbitrary")),
        name="attn_proj_route",
    )(o, x, mod, nw, w_o, wr_hi, wr_lo)


def _expert_kernel(be_ref, nu_ref, st_ref, stn_ref, tok_hbm, w1_ref, w3_ref, w2_ref, y_ref,
                   xf_s, xb_s, acc_s, sems, *, issue_rows):
    del be_ref
    b = pl.program_id(0)
    f = pl.program_id(1)
    bm = xb_s.shape[0]
    last = f == pl.num_programs(1) - 1
    n_used = nu_ref[0]
    used = b < n_used
    cur = b % 2

    def row_copy(tok, r, buf):
        return pltpu.make_async_copy(tok_hbm.at[pl.ds(tok, 1)], xf_s.at[buf, pl.ds(r, 1)], sems.at[buf])

    def request(idx_ref, buf, r0, n):
        def body(i, carry):
            for u in range(2):
                r = r0 + 2 * i + u
                row_copy(idx_ref[0, r], r, buf).start(priority=u)
            return carry
        lax.fori_loop(0, n // 2, body, 0)

    @pl.when(used)
    def _():
        @pl.when(jnp.logical_and(b == 0, f == 0))
        def _():
            request(st_ref, 0, 0, bm)

        @pl.when(jnp.logical_and(f < bm // issue_rows, b + 1 < n_used))
        def _():
            request(stn_ref, 1 - cur, f * issue_rows, issue_rows)

        @pl.when(f == 0)
        def _():
            def wait(i, carry):
                row_copy(0, 0, cur).wait()
                return carry
            lax.fori_loop(0, bm, wait, 0, unroll=8)
            xb_s[...] = xf_s[cur].astype(BF16)
            acc_s[...] = jnp.zeros_like(acc_s)

        xb = xb_s[...]
        hmid = (_silu(_dot(xb, w1_ref[...])) * _dot(xb, w3_ref[...])).astype(BF16)
        acc_s[...] += _dot(hmid, w2_ref[...])

        @pl.when(last)
        def _():
            y_ref[...] = acc_s[...]

    @pl.when(jnp.logical_and(jnp.logical_not(used), last))
    def _():
        y_ref[...] = jnp.zeros_like(y_ref)


def _experts(blk_e, n_used, slot_tok, toks, w1, w3, w2, *, bm, tf):
    n_blocks = slot_tok.shape[0]
    d = toks.shape[1]
    nf = w1.shape[2] // tf
    issue_rows = bm // min(nf, 8)

    def wcol(b, f, be, nu):
        return (be[b], 0, jnp.where(b < nu[0], f, nf - 1))

    def wrow(b, f, be, nu):
        return (be[b], jnp.where(b < nu[0], f, nf - 1), 0)

    return pl.pallas_call(
        functools.partial(_expert_kernel, issue_rows=issue_rows),
        grid_spec=pltpu.PrefetchScalarGridSpec(
            num_scalar_prefetch=2,
            grid=(n_blocks, nf),
            in_specs=[
                pl.BlockSpec((None, 1, bm), lambda b, f, be, nu: (b, 0, 0), memory_space=pltpu.SMEM),
                pl.BlockSpec((None, 1, bm), lambda b, f, be, nu: (jnp.minimum(b + 1, n_blocks - 1), 0, 0),
                             memory_space=pltpu.SMEM),
                pl.BlockSpec(memory_space=pl.ANY),
                pl.BlockSpec((None, d, tf), wcol),
                pl.BlockSpec((None, d, tf), wcol),
                pl.BlockSpec((None, tf, d), wrow),
            ],
            out_specs=pl.BlockSpec((bm, d), lambda b, f, be, nu: (b, 0)),
            scratch_shapes=[pltpu.VMEM((2, bm, d), F32), pltpu.VMEM((bm, d), BF16), pltpu.VMEM((bm, d), F32),
                            pltpu.SemaphoreType.DMA((2,))],
        ),
        out_shape=jax.ShapeDtypeStruct((n_blocks * bm, d), F32),
        compiler_params=_params(("arbitrary", "arbitrary")),
        name="moe_experts",
    )(blk_e, n_used, slot_tok, slot_tok, toks, w1, w3, w2)


def _combine_kernel(slot_ref, route_ref, x_ref, mod_ref, y_hbm, o_ref, y0_s, y1_s, sem):
    rows, d = x_ref.shape
    bufs = (y0_s, y1_s)

    def row_copy(s, k, r):
        return pltpu.make_async_copy(y_hbm.at[pl.ds(s, 1)], bufs[k].at[pl.ds(r, 1)], sem)

    def start(r, carry):
        for k in range(MOE_TOP_K):
            row_copy(slot_ref[0, 0, MOE_TOP_K * r + k], k, r).start(priority=k)
        return carry

    def wait(r, carry):
        for k in range(MOE_TOP_K):
            row_copy(0, k, 0).wait()
        return carry

    lax.fori_loop(0, rows, start, 0)
    lax.fori_loop(0, rows, wait, 0)
    _, _, gate2 = _mod_slices(mod_ref, pl.program_id(0), d, False)
    route = route_ref[...]
    f = route[:, 2:3] * y0_s[...] + route[:, 3:4] * y1_s[...]
    o_ref[...] = x_ref[...] + gate2 * f


def _combine(slot, route, x, mod, y, *, rows):
    nb, s, d = x.shape
    steps = s // rows
    return pl.pallas_call(
        _combine_kernel,
        grid=(nb, steps),
        in_specs=[
            pl.BlockSpec((1, 1, MOE_TOP_K * rows), lambda b, i: (b * steps + i, 0, 0), memory_space=pltpu.SMEM),
            pl.BlockSpec((None, rows, LANE), lambda b, i: (b, i, 0)),
            pl.BlockSpec((None, rows, d), lambda b, i: (b, i, 0)),
            pl.BlockSpec(mod.shape, lambda b, i: (0, 0)),
            pl.BlockSpec(memory_space=pl.ANY),
        ],
        out_specs=pl.BlockSpec((None, rows, d), lambda b, i: (b, i, 0)),
        out_shape=jax.ShapeDtypeStruct(x.shape, F32),
        scratch_shapes=[pltpu.VMEM((rows, d), F32), pltpu.VMEM((rows, d), F32), pltpu.SemaphoreType.DMA],
        compiler_params=_params(("arbitrary", "arbitrary")),
        name="moe_combine",
    )(slot.reshape(nb * steps, 1, MOE_TOP_K * rows), route, x, mod, y)


def _moe_plan(route, bm):
    n_tok = route.shape[0]
    e_flat = route[:, :MOE_TOP_K].astype(jnp.int32).reshape(-1)
    onehot = (e_flat[:, None] == jnp.arange(MOE_EXPERTS)[None, :]).astype(jnp.int32)
    csum = jnp.cumsum(onehot, axis=0)
    counts = csum[-1]
    rank = jnp.sum((csum - onehot) * onehot, axis=1)
    padded = (counts + bm - 1) // bm * bm
    pad_end = jnp.cumsum(padded)
    slot = (pad_end - padded)[e_flat] + rank
    n_blocks = (n_tok * MOE_TOP_K) // bm + MOE_EXPERTS
    blk_e = jnp.minimum(jnp.searchsorted(pad_end, jnp.arange(n_blocks) * bm, side="right"), MOE_EXPERTS - 1)
    slot = slot.astype(jnp.int32)
    slot_tok = jnp.zeros((n_blocks * bm,), jnp.int32).at[slot].set(
        jnp.arange(n_tok * MOE_TOP_K, dtype=jnp.int32) // MOE_TOP_K, unique_indices=True)
    return (slot, slot_tok.reshape(n_blocks, 1, bm), blk_e.astype(jnp.int32),
            (pad_end[-1:] // bm).astype(jnp.int32))


def kernel(x, c, ctx, c_ctx, w_ada, b_ada, norm_w, mlp_w_in, mlp_g_v, mlp_w_s, mlp_b_s, mlp_w_out,
           ffn_w1, ffn_w3, ffn_w2, na_w_qkv, na_g_q, na_g_k, na_rpb, na_w_o,
           moe_w_router, moe_w1, moe_w3, moe_w2):
    nb, s, d = x.shape
    n_ctx = ctx.shape[1]
    assert nb + 1 <= MOD_ROWS and w_ada.shape[0] == 2
    inner = mlp_w_out.shape[1]
    assert inner // MLP_GROUPS == LANE and d % NA_HEAD_DIM == 0

    cc = jnp.zeros((MOD_ROWS, d), F32).at[:nb].set(c).at[nb].set(c_ctx)
    mod = _ada_mod(cc, w_ada, b_ada)
    ctx_flat = ctx.reshape(1, nb * n_ctx, d)

    nw0 = norm_w[0]
    w_in = mlp_w_in[0].astype(BF16)
    w_s = mlp_w_s[0].astype(BF16)
    w_out = mlp_w_out[0].astype(BF16)
    g_v = mlp_g_v[0][None]
    b_full = jnp.repeat(mlp_b_s[0].T, inner // MLP_GROUPS, axis=1)
    mix = functools.partial(_mixer, mod=mod[0], nw=nw0[0][None], w_in=w_in, g_v=g_v, w_s=w_s, b_full=b_full,
                            w_out=w_out, tm=256)
    x1 = mix(x, mod_row=None)
    xc1 = mix(ctx_flat, mod_row=nb)
    ffn = functools.partial(_ffn, mod=mod[0], nw=nw0[1][None], w1=ffn_w1[0].astype(BF16),
                            w3=ffn_w3[0].astype(BF16), w2=ffn_w2[0].astype(BF16), tm=512, tf=512)
    x1 = ffn(x1, mod_row=None)
    xc1 = ffn(xc1, mod_row=nb)

    nw1 = norm_w[1]
    w_qkv = na_w_qkv[0].astype(BF16)
    qkv_fn = functools.partial(_qkv, mod=mod[1], nw=nw1[0][None], w_qkv=w_qkv, g_q=na_g_q[0][None],
                               g_k=na_g_k[0][None], tm=512)
    qkv = qkv_fn(x1, mod_row=None, j0=0)
    kvc = qkv_fn(xc1, mod_row=nb, j0=1)
    o = _attention(qkv, kvc, _bias_table(na_rpb[0]))

    wr = jnp.zeros((d, LANE), F32).at[:, :MOE_EXPERTS].set(moe_w_router[0])
    wr_hi, wr_lo = _split_bf16(wr)
    x2, toks, route = _proj_route(o, x1, mod[1], nw1[1][None], na_w_o[0].astype(BF16), wr_hi, wr_lo, tm=512)

    bm = 512
    n_tok = nb * s
    route_flat = route.reshape(n_tok, LANE)
    slot, slot_tok, blk_e, n_used = _moe_plan(route_flat, bm)
    ys = _experts(blk_e, n_used, slot_tok, toks.reshape(n_tok, d), moe_w1[0].astype(BF16),
                  moe_w3[0].astype(BF16), moe_w2[0].astype(BF16), bm=bm, tf=512)
    return _combine(slot, route, x2, mod[1], ys, rows=256)
```

```python
import functools

import numpy as np
import jax
import jax.numpy as jnp
from jax import lax
from jax.experimental import pallas as pl
from jax.experimental.pallas import tpu as pltpu

GRID_W = 64
N_MOD = 6
MLP_CHUNK = 128
MLP_GROUPS = 16
NA_HEAD_DIM = 128
NA_ROWS = 8
NA_COLS = 16
MOE_EXPERTS = 8
MOE_TOP_K = 2
EPS = 1e-6
NEG_INF = -1e30

LANE = 128
DMA_UNROLL = 8
MOD_ROWS = 16
VMEM_LIMIT = 56 * 1024 * 1024

F32 = jnp.float32
BF16 = jnp.bfloat16


def _dot(a, b):
    return jnp.dot(a, b, preferred_element_type=F32)


def _dot_t(a, b):
    return lax.dot_general(a, b, (((1,), (1,)), ((), ())), preferred_element_type=F32)


def _split_bf16(a):
    hi = a.astype(BF16)
    lo = (a - hi.astype(F32)).astype(BF16)
    return hi, lo


def _silu(a):
    return a * jax.nn.sigmoid(a)


def _norm_mod(x, nw, shift, scale):
    ms = jnp.mean(x * x, axis=-1, keepdims=True)
    y = x * lax.rsqrt(ms + EPS)
    return (y * nw) * (1.0 + scale) + shift


def _mod_slices(mod_ref, row, d, first):
    base = 0 if first else 3
    return tuple(mod_ref[pl.ds(row, 1), (base + k) * d:(base + k + 1) * d] for k in range(3))


def _params(sem):
    return pltpu.CompilerParams(dimension_semantics=sem, vmem_limit_bytes=VMEM_LIMIT)


def _side_rows(rows, n_steps):
    return next(r for r in range(16, rows + 1, 16) if rows % r == 0 and rows // r <= n_steps)


def _side_specs(side, n_steps, step_of):
    in_specs, out_specs, out_shapes = [], [], []
    for w in side:
        rows, cols = w.shape
        r = _side_rows(rows, n_steps)
        index = lambda *ids, last=rows // r - 1: (jnp.minimum(step_of(*ids), last), 0)
        in_specs.append(pl.BlockSpec((r, cols), index))
        out_specs.append(pl.BlockSpec((r, cols), index))
        out_shapes.append(jax.ShapeDtypeStruct(w.shape, BF16))
    return in_specs, out_specs, out_shapes


def _side_cast(side_in, side_out):
    for src, dst in zip(side_in, side_out):
        dst[...] = src[...].astype(BF16)


def _ada_kernel(a_ref, w_ref, b_ref, o_ref):
    a_hi, a_lo = _split_bf16(_silu(a_ref[...]))
    w_hi, w_lo = _split_bf16(w_ref[...])
    o_ref[...] = _dot(a_hi, w_hi) + _dot(a_lo, w_hi) + _dot(a_hi, w_lo) + b_ref[...]


def _ada_mod(cc, w_ada, b_ada, tn=1024):
    nl, d, n = w_ada.shape
    tn = min(tn, d)
    return pl.pallas_call(
        _ada_kernel,
        grid=(nl, n // tn),
        in_specs=[
            pl.BlockSpec((MOD_ROWS, d), lambda l, j: (0, 0)),
            pl.BlockSpec((None, d, tn), lambda l, j: (l, 0, j)),
            pl.BlockSpec((None, 1, tn), lambda l, j: (l, 0, j)),
        ],
        out_specs=pl.BlockSpec((None, MOD_ROWS, tn), lambda l, j: (l, 0, j)),
        out_shape=jax.ShapeDtypeStruct((nl, MOD_ROWS, n), F32),
        compiler_params=_params(("arbitrary", "arbitrary")),
        name="ada_mod",
    )(cc, w_ada, b_ada.reshape(nl, 1, n))


def _mixer_kernel(x_ref, mod_ref, nw_ref, win_ref, gv_ref, ws_ref, bs_ref, wout_ref, *rest,
                  mod_row, col_chunk, n_side):
    side_in, (o_ref, *side_out), (u_s, v_s, m_s) = rest[:n_side], rest[n_side:2 * n_side + 1], rest[2 * n_side + 1:]
    _side_cast(side_in, side_out)
    tm, d = x_ref.shape
    inner = u_s.shape[1]
    gd = inner // MLP_GROUPS
    row = pl.program_id(0) if mod_row is None else mod_row
    sh, sc, gate = _mod_slices(mod_ref, row, d, True)
    x = x_ref[...]
    h = _norm_mod(x, nw_ref[...], sh, sc).astype(BF16)
    ss = jnp.zeros((tm, 1), F32)
    for c in range(2 * inner // col_chunk):
        lo = c * col_chunk
        z = _dot(h, win_ref[:, lo:lo + col_chunk])
        z = 0.5 * z * (1.0 + lax.erf(z * (2.0 ** -0.5)))
        if lo < inner:
            u_s[:, lo:lo + col_chunk] = z
        else:
            v_s[:, lo - inner:lo - inner + col_chunk] = z
            ss = ss + jnp.sum(z * z, axis=-1, keepdims=True)
    rinv = lax.rsqrt(ss / inner + EPS)
    for n in range(tm // MLP_CHUNK):
        rows = slice(n * MLP_CHUNK, (n + 1) * MLP_CHUNK)
        for g in range(MLP_GROUPS):
            cols = slice(g * gd, (g + 1) * gd)
            vb = ((v_s[rows, cols] * rinv[rows]) * gv_ref[:, cols]).astype(BF16)
            mixed = _dot(ws_ref[g], vb) + bs_ref[:, cols]
            m_s[rows, cols] = (u_s[rows, cols] * mixed).astype(BF16)
    y = _dot(m_s[...], wout_ref[...])
    o_ref[...] = x + gate * y


def _mixer(x, mod, nw, w_in, g_v, w_s, b_full, w_out, *, mod_row, tm, side=()):
    nb, s, d = x.shape
    inner = w_out.shape[0]
    tm = min(tm, s)
    ni = s // tm
    const = lambda *shape: pl.BlockSpec(shape, lambda b, i: (0,) * len(shape))
    side_in, side_out, side_shapes = _side_specs(side, nb * ni, lambda b, i: b * ni + i)
    return pl.pallas_call(
        functools.partial(_mixer_kernel, mod_row=mod_row, col_chunk=min(512, inner), n_side=len(side)),
        grid=(nb, ni),
        in_specs=[
            pl.BlockSpec((None, tm, d), lambda b, i: (b, i, 0)),
            const(*mod.shape), const(1, d), const(d, 2 * inner), const(1, inner),
            const(*w_s.shape), const(MLP_CHUNK, inner), const(inner, d), *side_in,
        ],
        out_specs=[pl.BlockSpec((None, tm, d), lambda b, i: (b, i, 0)), *side_out],
        out_shape=[jax.ShapeDtypeStruct(x.shape, F32), *side_shapes],
        scratch_shapes=[pltpu.VMEM((tm, inner), F32), pltpu.VMEM((tm, inner), F32),
                        pltpu.VMEM((tm, inner), BF16)],
        compiler_params=_params(("arbitrary", "arbitrary")),
        name="gmlp_mixer",
    )(x, mod, nw, w_in, g_v, w_s, b_full, w_out, *side)


def _ffn_kernel(x_ref, mod_ref, nw_ref, w1_ref, w3_ref, w2_ref, *rest, mod_row, n_side):
    side_in, (o_ref, *side_out), (t_s, acc_s) = rest[:n_side], rest[n_side:2 * n_side + 1], rest[2 * n_side + 1:]
    _side_cast(side_in, side_out)
    d = x_ref.shape[-1]
    f = pl.program_id(2)
    row = pl.program_id(0) if mod_row is None else mod_row
    sh, sc, gate = _mod_slices(mod_ref, row, d, False)

    @pl.when(f == 0)
    def _():
        t_s[...] = _norm_mod(x_ref[...], nw_ref[...], sh, sc).astype(BF16)
        acc_s[...] = jnp.zeros_like(acc_s)

    t = t_s[...]
    hmid = (_silu(_dot(t, w1_ref[...])) * _dot(t, w3_ref[...])).astype(BF16)
    acc_s[...] += _dot(hmid, w2_ref[...])

    @pl.when(f == pl.num_programs(2) - 1)
    def _():
        o_ref[...] = x_ref[...] + gate * acc_s[...]


def _ffn(x, mod, nw, w1, w3, w2, *, mod_row, tm, tf, side=()):
    nb, s, d = x.shape
    tm = min(tm, s)
    ni = s // tm
    nf = w1.shape[1] // tf
    side_in, side_out, side_shapes = _side_specs(side, nb * ni * nf, lambda b, i, f: (b * ni + i) * nf + f)
    return pl.pallas_call(
        functools.partial(_ffn_kernel, mod_row=mod_row, n_side=len(side)),
        grid=(nb, ni, nf),
        in_specs=[
            pl.BlockSpec((None, tm, d), lambda b, i, f: (b, i, 0)),
            pl.BlockSpec(mod.shape, lambda b, i, f: (0, 0)),
            pl.BlockSpec((1, d), lambda b, i, f: (0, 0)),
            pl.BlockSpec((d, tf), lambda b, i, f: (0, f)),
            pl.BlockSpec((d, tf), lambda b, i, f: (0, f)),
            pl.BlockSpec((tf, d), lambda b, i, f: (f, 0)),
            *side_in,
        ],
        out_specs=[pl.BlockSpec((None, tm, d), lambda b, i, f: (b, i, 0)), *side_out],
        out_shape=[jax.ShapeDtypeStruct(x.shape, F32), *side_shapes],
        scratch_shapes=[pltpu.VMEM((tm, d), BF16), pltpu.VMEM((tm, d), F32)],
        compiler_params=_params(("arbitrary", "arbitrary", "arbitrary")),
        name="ffn_swiglu",
    )(x, mod, nw, w1, w3, w2, *side)


def _qkv_kernel(x_ref, mod_ref, nw_ref, w_ref, gq_ref, gk_ref, o_ref, h_s, *, mod_row, j0):
    d = x_ref.shape[-1]
    jj = pl.program_id(2)
    j = jj + j0
    row = pl.program_id(0) if mod_row is None else mod_row
    sh, sc, _ = _mod_slices(mod_ref, row, d, True)

    @pl.when(jj == 0)
    def _():
        h_s[...] = _norm_mod(x_ref[...], nw_ref[...], sh, sc).astype(BF16)

    y = _dot(h_s[...], w_ref[...])

    @pl.when(j == 2)
    def _():
        o_ref[...] = y.astype(BF16)

    @pl.when(j < 2)
    def _():
        is_q = j == 0
        g = jnp.where(is_q, gq_ref[...], gk_ref[...])
        post = jnp.where(is_q, NA_HEAD_DIM ** -0.5, 1.0).astype(F32)
        for hh in range(d // NA_HEAD_DIM):
            cols = slice(hh * NA_HEAD_DIM, (hh + 1) * NA_HEAD_DIM)
            blk = y[:, cols]
            ms = jnp.mean(blk * blk, axis=-1, keepdims=True)
            o_ref[:, cols] = (((blk * lax.rsqrt(ms + EPS)) * g) * post).astype(BF16)


def _qkv(x, mod, nw, w_qkv, g_q, g_k, *, mod_row, tm, j0):
    nb, s, d = x.shape
    tm = min(tm, s)
    nj = 3 - j0
    return pl.pallas_call(
        functools.partial(_qkv_kernel, mod_row=mod_row, j0=j0),
        grid=(nb, s // tm, nj),
        in_specs=[
            pl.BlockSpec((None, tm, d), lambda b, i, j: (b, i, 0)),
            pl.BlockSpec(mod.shape, lambda b, i, j: (0, 0)),
            pl.BlockSpec((1, d), lambda b, i, j: (0, 0)),
            pl.BlockSpec((d, d), lambda b, i, j: (0, j + j0)),
            pl.BlockSpec((1, NA_HEAD_DIM), lambda b, i, j: (0, 0)),
            pl.BlockSpec((1, NA_HEAD_DIM), lambda b, i, j: (0, 0)),
        ],
        out_specs=pl.BlockSpec((None, None, tm, d), lambda b, i, j: (j, b, i, 0)),
        out_shape=jax.ShapeDtypeStruct((nj, nb, s, d), BF16),
        scratch_shapes=[pltpu.VMEM((tm, d), BF16)],
        compiler_params=_params(("arbitrary", "arbitrary", "arbitrary")),
        name="qkv_proj",
    )(x, mod, nw, w_qkv, g_q, g_k)


def _bias_table(rpb):
    qcol = np.arange(GRID_W)[:, None]
    kcol = np.arange(GRID_W)[None, :]
    cs = np.clip(qcol - NA_COLS // 2, 0, GRID_W - NA_COLS)
    inwin = (kcol >= cs) & (kcol < cs + NA_COLS)
    dc = np.clip(kcol - qcol + NA_COLS - 1, 0, 2 * NA_COLS - 2)
    tab = jnp.where(inwin[None, None], rpb.astype(F32)[:, :, dc], NEG_INF)
    return jnp.concatenate([tab[:, :-1], tab[:, 1:]], axis=-1)


def _attn_kernel(q_ref, k_ref, v_ref, kc_ref, vc_ref, tab_ref, o_ref, s_s, p_s, *, kh):
    d = q_ref.shape[-1]
    heads = d // NA_HEAD_DIM
    rows = k_ref.shape[0] // GRID_W
    n_loc = kh * GRID_W
    r = pl.program_id(1)
    rs = jnp.clip(r - kh // 2, 0, rows - kh)
    dr0 = rs - r + (NA_ROWS - 1)
    start = pl.multiple_of(rs * GRID_W, GRID_W)
    win = pl.ds(start, n_loc)
    head_cols = [slice(hh * NA_HEAD_DIM, (hh + 1) * NA_HEAD_DIM) for hh in range(heads)]
    for hh, cols in enumerate(head_cols):
        qh = q_ref[:, cols]
        bias = jnp.concatenate([tab_ref[hh, pl.ds(dr0 + 2 * j, 1)][0] for j in range(kh // 2)], axis=-1)
        s_s[hh, :, :n_loc] = _dot_t(qh, k_ref[win, cols]) + bias
        s_s[hh, :, n_loc:] = _dot_t(qh, kc_ref[:, cols])
    dens = []
    for hh in range(heads):
        s = s_s[hh]
        p = jnp.exp(s - jnp.max(s, axis=-1, keepdims=True))
        dens.append(jnp.sum(p, axis=-1, keepdims=True))
        p_s[hh] = p.astype(BF16)
    for hh, cols in enumerate(head_cols):
        o = _dot(p_s[hh, :, :n_loc], v_ref[win, cols]) + _dot(p_s[hh, :, n_loc:], vc_ref[:, cols])
        o_ref[:, cols] = (o / dens[hh]).astype(BF16)


def _attention(qkv, kvc, tab):
    _, nb, s, d = qkv.shape
    n_ctx = kvc.shape[2] // nb
    rows = s // GRID_W
    kh = min(NA_ROWS, rows)
    once = pl.Buffered(1)
    return pl.pallas_call(
        functools.partial(_attn_kernel, kh=kh),
        grid=(nb, rows),
        in_specs=[
            pl.BlockSpec((None, None, GRID_W, d), lambda b, r: (0, b, r, 0)),
            pl.BlockSpec((None, None, s, d), lambda b, r: (1, b, 0, 0), pipeline_mode=once),
            pl.BlockSpec((None, None, s, d), lambda b, r: (2, b, 0, 0), pipeline_mode=once),
            pl.BlockSpec((None, None, n_ctx, d), lambda b, r: (0, 0, b, 0)),
            pl.BlockSpec((None, None, n_ctx, d), lambda b, r: (1, 0, b, 0)),
            pl.BlockSpec(tab.shape, lambda b, r: (0, 0, 0, 0), pipeline_mode=once),
        ],
        out_specs=pl.BlockSpec((None, GRID_W, d), lambda b, r: (b, r, 0)),
        out_shape=jax.ShapeDtypeStruct((nb, s, d), BF16),
        scratch_shapes=[pltpu.VMEM((d // NA_HEAD_DIM, GRID_W, kh * GRID_W + n_ctx), F32),
                        pltpu.VMEM((d // NA_HEAD_DIM, GRID_W, kh * GRID_W + n_ctx), BF16)],
        compiler_params=_params(("arbitrary", "arbitrary")),
        name="nbr_attention",
    )(qkv, qkv, qkv, kvc, kvc, tab)


def _proj_route_kernel(o_ref, x_ref, mod_ref, nw_ref, wo_ref, wrh_ref, wrl_ref, x2_ref, tok_ref, route_ref):
    tm, d = x_ref.shape
    row = pl.program_id(0)
    _, _, gate1 = _mod_slices(mod_ref, row, d, True)
    sh, sc, _ = _mod_slices(mod_ref, row, d, False)
    x2 = x_ref[...] + gate1 * _dot(o_ref[...], wo_ref[...])
    x2_ref[...] = x2
    t = _norm_mod(x2, nw_ref[...], sh, sc)
    tok_ref[...] = t
    t_hi, t_lo = _split_bf16(t)
    logits = _dot(t_hi, wrh_ref[...]) + _dot(t_lo, wrh_ref[...]) + _dot(t_hi, wrl_ref[...])
    lane = lax.broadcasted_iota(jnp.int32, (tm, LANE), 1)
    logits = jnp.where(lane < MOE_EXPERTS, logits, -jnp.inf)
    m1 = jnp.max(logits, axis=-1, keepdims=True)
    i1 = jnp.min(jnp.where(logits == m1, lane, LANE), axis=-1, keepdims=True)
    rest = jnp.where(lane == i1, -jnp.inf, logits)
    m2 = jnp.max(rest, axis=-1, keepdims=True)
    i2 = jnp.min(jnp.where(rest == m2, lane, LANE), axis=-1, keepdims=True)
    e = jnp.exp(m2 - m1)
    den = 1.0 + e
    route = jnp.where(lane == 0, i1.astype(F32),
                      jnp.where(lane == 1, i2.astype(F32),
                                jnp.where(lane == 2, 1.0 / den, jnp.where(lane == 3, e / den, 0.0))))
    route_ref[...] = route


def _proj_route(o, x, mod, nw, w_o, wr_hi, wr_lo, *, tm):
    nb, s, d = x.shape
    const = lambda *shape: pl.BlockSpec(shape, lambda b, i: (0,) * len(shape))
    tile = lambda w: pl.BlockSpec((None, tm, w), lambda b, i: (b, i, 0))
    return pl.pallas_call(
        _proj_route_kernel,
        grid=(nb, s // tm),
        in_specs=[tile(d), tile(d), const(*mod.shape), const(1, d), const(d, d), const(d, LANE), const(d, LANE)],
        out_specs=[tile(d), tile(d), tile(LANE)],
        out_shape=[jax.ShapeDtypeStruct(x.shape, F32), jax.ShapeDtypeStruct(x.shape, F32),
                   jax.ShapeDtypeStruct((nb, s, LANE), F32)],
        compiler_params=_params(("arbitrary", "arbitrary")),
        name="attn_proj_route",
    )(o, x, mod, nw, w_o, wr_hi, wr_lo)


def _expert_kernel(be_ref, nu_ref, st_ref, stn_ref, tok_hbm, w1_ref, w3_ref, w2_ref, y_ref,
                   xf_s, xb_s, acc_s, sems, *, issue_rows):
    del be_ref
    b = pl.program_id(0)
    f = pl.program_id(1)
    bm = xb_s.shape[0]
    last = f == pl.num_programs(1) - 1
    n_used = nu_ref[0]
    used = b < n_used
    cur = b % 2

    def row_copy(tok, r, buf):
        return pltpu.make_async_copy(tok_hbm.at[pl.ds(tok, 1)], xf_s.at[buf, pl.ds(r, 1)], sems.at[buf])

    def request(idx_ref, buf, r0, n):
        def body(i, carry):
            for u in range(DMA_UNROLL):
                r = r0 + DMA_UNROLL * i + u
                row_copy(idx_ref[0, r], r, buf).start(priority=u % 2)
            return carry
        lax.fori_loop(0, n // DMA_UNROLL, body, 0)

    @pl.when(used)
    def _():
        @pl.when(jnp.logical_and(b == 0, f == 0))
        def _():
            request(st_ref, 0, 0, bm)

        @pl.when(jnp.logical_and(f < bm // issue_rows, b + 1 < n_used))
        def _():
            request(stn_ref, 1 - cur, f * issue_rows, issue_rows)

        @pl.when(f == 0)
        def _():
            def wait(i, carry):
                row_copy(0, 0, cur).wait()
                return carry
            lax.fori_loop(0, bm, wait, 0, unroll=8)
            xb_s[...] = xf_s[cur].astype(BF16)
            acc_s[...] = jnp.zeros_like(acc_s)

        xb = xb_s[...]
        hmid = (_silu(_dot(xb, w1_ref[...])) * _dot(xb, w3_ref[...])).astype(BF16)
        acc_s[...] += _dot(hmid, w2_ref[...])

        @pl.when(last)
        def _():
            y_ref[...] = acc_s[...]

    @pl.when(jnp.logical_and(jnp.logical_not(used), last))
    def _():
        y_ref[...] = jnp.zeros_like(y_ref)


def _experts(blk_e, n_used, slot_tok, toks, w1, w3, w2, *, bm, tf):
    n_blocks = slot_tok.shape[0]
    d = toks.shape[1]
    nf = w1.shape[2] // tf
    issue_steps = next(n for n in (8, 4, 2, 1) if n <= nf)
    issue_rows = bm // issue_steps
    assert issue_rows % DMA_UNROLL == 0

    def wcol(b, f, be, nu):
        return (be[b], 0, jnp.where(b < nu[0], f, nf - 1))

    def wrow(b, f, be, nu):
        return (be[b], jnp.where(b < nu[0], f, nf - 1), 0)

    return pl.pallas_call(
        functools.partial(_expert_kernel, issue_rows=issue_rows),
        grid_spec=pltpu.PrefetchScalarGridSpec(
            num_scalar_prefetch=2,
            grid=(n_blocks, nf),
            in_specs=[
                pl.BlockSpec((None, 1, bm), lambda b, f, be, nu: (b, 0, 0), memory_space=pltpu.SMEM),
                pl.BlockSpec((None, 1, bm), lambda b, f, be, nu: (jnp.minimum(b + 1, n_blocks - 1), 0, 0),
                             memory_space=pltpu.SMEM),
                pl.BlockSpec(memory_space=pl.ANY),
                pl.BlockSpec((None, d, tf), wcol),
                pl.BlockSpec((None, d, tf), wcol),
                pl.BlockSpec((None, tf, d), wrow),
            ],
            out_specs=pl.BlockSpec((bm, d), lambda b, f, be, nu: (b, 0)),
            scratch_shapes=[pltpu.VMEM((2, bm, d), F32), pltpu.VMEM((bm, d), BF16), pltpu.VMEM((bm, d), F32),
                            pltpu.SemaphoreType.DMA((2,))],
        ),
        out_shape=jax.ShapeDtypeStruct((n_blocks * bm, d), F32),
        compiler_params=_params(("arbitrary", "arbitrary")),
        name="moe_experts",
    )(blk_e, n_used, slot_tok, slot_tok, toks, w1, w3, w2)


def _combine_kernel(slot_ref, slotn_ref, route_ref, x_ref, mod_ref, y_hbm, o_ref, y_s, sems):
    rows, d = x_ref.shape
    step = pl.program_id(0) * pl.num_programs(1) + pl.program_id(1)
    n_steps = pl.num_programs(0) * pl.num_programs(1)
    cur = step % 2
    tok_unroll = DMA_UNROLL // MOE_TOP_K

    def row_copy(s, buf, k, r):
        return pltpu.make_async_copy(y_hbm.at[pl.ds(s, 1)], y_s.at[buf, k, pl.ds(r, 1)], sems.at[buf])

    def request(idx_ref, buf):
        def body(i, carry):
            for u in range(tok_unroll):
                r = tok_unroll * i + u
                for k in range(MOE_TOP_K):
                    row_copy(idx_ref[0, 0, MOE_TOP_K * r + k], buf, k, r).start(priority=k)
            return carry
        lax.fori_loop(0, rows // tok_unroll, body, 0)

    @pl.when(step == 0)
    def _():
        request(slot_ref, 0)

    @pl.when(step + 1 < n_steps)
    def _():
        request(slotn_ref, 1 - cur)

    def wait(i, carry):
        row_copy(0, cur, 0, 0).wait()
        return carry

    lax.fori_loop(0, rows * MOE_TOP_K, wait, 0, unroll=8)
    _, _, gate2 = _mod_slices(mod_ref, pl.program_id(0), d, False)
    route = route_ref[...]
    f = route[:, 2:3] * y_s[cur, 0] + route[:, 3:4] * y_s[cur, 1]
    o_ref[...] = x_ref[...] + gate2 * f


def _combine(slot, route, x, mod, y, *, rows):
    nb, s, d = x.shape
    steps = s // rows
    idx_spec = lambda ahead: pl.BlockSpec(
        (1, 1, MOE_TOP_K * rows), lambda b, i: (jnp.minimum(b * steps + i + ahead, nb * steps - 1), 0, 0),
        memory_space=pltpu.SMEM)
    slot = slot.reshape(nb * steps, 1, MOE_TOP_K * rows)
    return pl.pallas_call(
        _combine_kernel,
        grid=(nb, steps),
        in_specs=[
            idx_spec(0), idx_spec(1),
            pl.BlockSpec((None, rows, LANE), lambda b, i: (b, i, 0)),
            pl.BlockSpec((None, rows, d), lambda b, i: (b, i, 0)),
            pl.BlockSpec(mod.shape, lambda b, i: (0, 0)),
            pl.BlockSpec(memory_space=pl.ANY),
        ],
        out_specs=pl.BlockSpec((None, rows, d), lambda b, i: (b, i, 0)),
        out_shape=jax.ShapeDtypeStruct(x.shape, F32),
        scratch_shapes=[pltpu.VMEM((2, MOE_TOP_K, rows, d), F32), pltpu.SemaphoreType.DMA((2,))],
        compiler_params=_params(("arbitrary", "arbitrary")),
        name="moe_combine",
    )(slot, slot, route, x, mod, y)


def _moe_plan(route, bm):
    n_tok = route.shape[0]
    e_flat = route[:, :MOE_TOP_K].astype(jnp.int32).reshape(-1)
    onehot = (e_flat[:, None] == jnp.arange(MOE_EXPERTS)[None, :]).astype(jnp.int32)
    csum = jnp.cumsum(onehot, axis=0)
    counts = csum[-1]
    rank = jnp.sum((csum - onehot) * onehot, axis=1)
    padded = (counts + bm - 1) // bm * bm
    pad_end = jnp.cumsum(padded)
    slot = (pad_end - padded)[e_flat] + rank
    n_blocks = (n_tok * MOE_TOP_K) // bm + MOE_EXPERTS
    blk_e = jnp.minimum(jnp.searchsorted(pad_end, jnp.arange(n_blocks) * bm, side="right"), MOE_EXPERTS - 1)
    slot = slot.astype(jnp.int32)
    slot_tok = jnp.zeros((n_blocks * bm,), jnp.int32).at[slot].set(
        jnp.arange(n_tok * MOE_TOP_K, dtype=jnp.int32) // MOE_TOP_K, unique_indices=True)
    return (slot, slot_tok.reshape(n_blocks, 1, bm), blk_e.astype(jnp.int32),
            (pad_end[-1:] // bm).astype(jnp.int32))


def kernel(x, c, ctx, c_ctx, w_ada, b_ada, norm_w, mlp_w_in, mlp_g_v, mlp_w_s, mlp_b_s, mlp_w_out,
           ffn_w1, ffn_w3, ffn_w2, na_w_qkv, na_g_q, na_g_k, na_rpb, na_w_o,
           moe_w_router, moe_w1, moe_w3, moe_w2):
    nb, s, d = x.shape
    n_ctx = ctx.shape[1]
    assert nb + 1 <= MOD_ROWS and w_ada.shape[0] == 2
    inner = mlp_w_out.shape[1]
    assert inner // MLP_GROUPS == LANE and d % NA_HEAD_DIM == 0

    cc = jnp.zeros((MOD_ROWS, d), F32).at[:nb].set(c).at[nb].set(c_ctx)
    mod = _ada_mod(cc, w_ada, b_ada)
    ctx_flat = ctx.reshape(1, nb * n_ctx, d)

    nw0 = norm_w[0]
    w_in = mlp_w_in[0].astype(BF16)
    w_s = mlp_w_s[0].astype(BF16)
    w_out = mlp_w_out[0].astype(BF16)
    g_v = mlp_g_v[0][None]
    b_full = jnp.repeat(mlp_b_s[0].T, inner // MLP_GROUPS, axis=1)
    mix = functools.partial(_mixer, mod=mod[0], nw=nw0[0][None], w_in=w_in, g_v=g_v, w_s=w_s, b_full=b_full,
                            w_out=w_out, tm=256)
    n_exp, _, moe_dim = moe_w1.shape[1:]
    x1, e_w2 = mix(x, mod_row=None, side=(moe_w2[0].reshape(n_exp * moe_dim, d),))
    xc1, = mix(ctx_flat, mod_row=nb)
    ffn = functools.partial(_ffn, mod=mod[0], nw=nw0[1][None], w1=ffn_w1[0].astype(BF16),
                            w3=ffn_w3[0].astype(BF16), w2=ffn_w2[0].astype(BF16), tm=512, tf=512)
    x1, e_w1, e_w3 = ffn(x1, mod_row=None, side=(moe_w1[0].reshape(n_exp * d, moe_dim),
                                                  moe_w3[0].reshape(n_exp * d, moe_dim)))
    xc1, = ffn(xc1, mod_row=nb)

    nw1 = norm_w[1]
    w_qkv = na_w_qkv[0].astype(BF16)
    qkv_fn = functools.partial(_qkv, mod=mod[1], nw=nw1[0][None], w_qkv=w_qkv, g_q=na_g_q[0][None],
                               g_k=na_g_k[0][None], tm=512)
    qkv = qkv_fn(x1, mod_row=None, j0=0)
    kvc = qkv_fn(xc1, mod_row=nb, j0=1)
    o = _attention(qkv, kvc, _bias_table(na_rpb[0]))

    wr = jnp.zeros((d, LANE), F32).at[:, :MOE_EXPERTS].set(moe_w_router[0])
    wr_hi, wr_lo = _split_bf16(wr)
    x2, toks, route = _proj_route(o, x1, mod[1], nw1[1][None], na_w_o[0].astype(BF16), wr_hi, wr_lo, tm=512)

    bm = 512
    n_tok = nb * s
    route_flat = route.reshape(n_tok, LANE)
    slot, slot_tok, blk_e, n_used = _moe_plan(route_flat, bm)
    ys = _experts(blk_e, n_used, slot_tok, toks.reshape(n_tok, d), e_w1.reshape(n_exp, d, moe_dim),
                  e_w3.reshape(n_exp, d, moe_dim), e_w2.reshape(n_exp, moe_dim, d), bm=bm, tf=min(1024, moe_dim))
    return _combine(slot, route, x2, mod[1], ys, rows=256)
```

```python
import functools

import numpy as np
import jax
import jax.numpy as jnp
from jax import lax
from jax.experimental import pallas as pl
from jax.experimental.pallas import tpu as pltpu

GRID_W = 64
N_MOD = 6
MLP_CHUNK = 128
MLP_GROUPS = 16
NA_HEAD_DIM = 128
NA_ROWS = 8
NA_COLS = 16
MOE_EXPERTS = 8
MOE_TOP_K = 2
EPS = 1e-6
NEG_INF = -1e30

LANE = 128
SUB_ROWS = 128
DMA_UNROLL = 8
MOD_ROWS = 16
VMEM_LIMIT = 56 * 1024 * 1024

F32 = jnp.float32
BF16 = jnp.bfloat16


def _dot(a, b):
    return jnp.dot(a, b, preferred_element_type=F32)


def _dot_t(a, b):
    return lax.dot_general(a, b, (((1,), (1,)), ((), ())), preferred_element_type=F32)


def _split_bf16(a):
    hi = a.astype(BF16)
    lo = (a - hi.astype(F32)).astype(BF16)
    return hi, lo


def _silu(a):
    return a * jax.nn.sigmoid(a)


def _norm_mod(x, nw, shift, scale):
    ms = jnp.mean(x * x, axis=-1, keepdims=True)
    y = x * lax.rsqrt(ms + EPS)
    return (y * nw) * (1.0 + scale) + shift


def _mod_slices(mod_ref, row, d, first):
    base = 0 if first else 3
    return tuple(mod_ref[pl.ds(row, 1), (base + k) * d:(base + k + 1) * d] for k in range(3))


def _skewed(stages, n):
    state = [None] * n
    for t in range(n + len(stages) - 1):
        for s in reversed(range(len(stages))):
            k = t - s
            if 0 <= k < n:
                state[k] = stages[s](k, state[k])


def _params(sem):
    return pltpu.CompilerParams(dimension_semantics=sem, vmem_limit_bytes=VMEM_LIMIT)


def _side_rows(rows, n_steps):
    return next(r for r in range(16, rows + 1, 16) if rows % r == 0 and rows // r <= n_steps)


def _side_specs(side, n_steps, step_of):
    in_specs, out_specs, out_shapes = [], [], []
    for w in side:
        rows, cols = w.shape
        r = _side_rows(rows, n_steps)
        index = lambda *ids, last=rows // r - 1: (jnp.minimum(step_of(*ids), last), 0)
        in_specs.append(pl.BlockSpec((r, cols), index))
        out_specs.append(pl.BlockSpec((r, cols), index))
        out_shapes.append(jax.ShapeDtypeStruct(w.shape, BF16))
    return in_specs, out_specs, out_shapes


def _side_cast(side_in, side_out):
    for src, dst in zip(side_in, side_out):
        dst[...] = src[...].astype(BF16)


def _ada_kernel(a_ref, w_ref, b_ref, o_ref):
    a_hi, a_lo = _split_bf16(_silu(a_ref[...]))
    w_hi, w_lo = _split_bf16(w_ref[...])
    o_ref[...] = _dot(a_hi, w_hi) + _dot(a_lo, w_hi) + _dot(a_hi, w_lo) + b_ref[...]


def _ada_mod(cc, w_ada, b_ada, tn=1024):
    nl, d, n = w_ada.shape
    tn = min(tn, d)
    return pl.pallas_call(
        _ada_kernel,
        grid=(nl, n // tn),
        in_specs=[
            pl.BlockSpec((MOD_ROWS, d), lambda l, j: (0, 0)),
            pl.BlockSpec((None, d, tn), lambda l, j: (l, 0, j)),
            pl.BlockSpec((None, 1, tn), lambda l, j: (l, 0, j)),
        ],
        out_specs=pl.BlockSpec((None, MOD_ROWS, tn), lambda l, j: (l, 0, j)),
        out_shape=jax.ShapeDtypeStruct((nl, MOD_ROWS, n), F32),
        compiler_params=_params(("arbitrary", "arbitrary")),
        name="ada_mod",
    )(cc, w_ada, b_ada.reshape(nl, 1, n))


def _mixer_kernel(x_ref, mod_ref, nw_ref, win_ref, gv_ref, ws_ref, bs_ref, wout_ref, *rest,
                  mod_row, col_chunk, n_side):
    side_in, (o_ref, *side_out), (u_s, v_s, m_s) = rest[:n_side], rest[n_side:2 * n_side + 1], rest[2 * n_side + 1:]
    _side_cast(side_in, side_out)
    tm, d = x_ref.shape
    inner = u_s.shape[1]
    gd = inner // MLP_GROUPS
    row = pl.program_id(0) if mod_row is None else mod_row
    sh, sc, gate = _mod_slices(mod_ref, row, d, True)
    x = x_ref[...]
    h = _norm_mod(x, nw_ref[...], sh, sc).astype(BF16)
    ss = jnp.zeros((tm, 1), F32)
    for c in range(2 * inner // col_chunk):
        lo = c * col_chunk
        z = _dot(h, win_ref[:, lo:lo + col_chunk])
        z = 0.5 * z * (1.0 + lax.erf(z * (2.0 ** -0.5)))
        if lo < inner:
            u_s[:, lo:lo + col_chunk] = z
        else:
            v_s[:, lo - inner:lo - inner + col_chunk] = z
            ss = ss + jnp.sum(z * z, axis=-1, keepdims=True)
    rinv = lax.rsqrt(ss / inner + EPS)
    for n in range(tm // MLP_CHUNK):
        rows = slice(n * MLP_CHUNK, (n + 1) * MLP_CHUNK)
        for g in range(MLP_GROUPS):
            cols = slice(g * gd, (g + 1) * gd)
            vb = ((v_s[rows, cols] * rinv[rows]) * gv_ref[:, cols]).astype(BF16)
            mixed = _dot(ws_ref[g], vb) + bs_ref[:, cols]
            m_s[rows, cols] = (u_s[rows, cols] * mixed).astype(BF16)
    y = _dot(m_s[...], wout_ref[...])
    o_ref[...] = x + gate * y


def _mixer(x, mod, nw, w_in, g_v, w_s, b_full, w_out, *, mod_row, tm, side=()):
    nb, s, d = x.shape
    inner = w_out.shape[0]
    tm = min(tm, s)
    ni = s // tm
    const = lambda *shape: pl.BlockSpec(shape, lambda b, i: (0,) * len(shape))
    side_in, side_out, side_shapes = _side_specs(side, nb * ni, lambda b, i: b * ni + i)
    return pl.pallas_call(
        functools.partial(_mixer_kernel, mod_row=mod_row, col_chunk=min(512, inner), n_side=len(side)),
        grid=(nb, ni),
        in_specs=[
            pl.BlockSpec((None, tm, d), lambda b, i: (b, i, 0)),
            const(*mod.shape), const(1, d), const(d, 2 * inner), const(1, inner),
            const(*w_s.shape), const(MLP_CHUNK, inner), const(inner, d), *side_in,
        ],
        out_specs=[pl.BlockSpec((None, tm, d), lambda b, i: (b, i, 0)), *side_out],
        out_shape=[jax.ShapeDtypeStruct(x.shape, F32), *side_shapes],
        scratch_shapes=[pltpu.VMEM((tm, inner), F32), pltpu.VMEM((tm, inner), F32),
                        pltpu.VMEM((tm, inner), BF16)],
        compiler_params=_params(("arbitrary", "arbitrary")),
        name="gmlp_mixer",
    )(x, mod, nw, w_in, g_v, w_s, b_full, w_out, *side)


def _ffn_kernel(x_ref, mod_ref, nw_ref, w1_ref, w3_ref, w2_ref, *rest, mod_row, n_side):
    side_in, (o_ref, *side_out), (t_s, acc_s) = rest[:n_side], rest[n_side:2 * n_side + 1], rest[2 * n_side + 1:]
    _side_cast(side_in, side_out)
    d = x_ref.shape[-1]
    f = pl.program_id(2)
    row = pl.program_id(0) if mod_row is None else mod_row
    sh, sc, gate = _mod_slices(mod_ref, row, d, False)

    def up(t):
        return (_silu(_dot(t, w1_ref[...])) * _dot(t, w3_ref[...])).astype(BF16)

    @pl.when(f == 0)
    def _():
        sub = min(SUB_ROWS, x_ref.shape[0])

        def norm(k, _):
            rows = pl.ds(k * sub, sub)
            t = _norm_mod(x_ref[rows, :], nw_ref[...], sh, sc).astype(BF16)
            t_s[rows, :] = t
            return t

        def down(k, hmid):
            acc_s[pl.ds(k * sub, sub), :] = _dot(hmid, w2_ref[...])

        _skewed((norm, lambda k, t: up(t), down), x_ref.shape[0] // sub)

    @pl.when(f > 0)
    def _():
        acc_s[...] += _dot(up(t_s[...]), w2_ref[...])

    @pl.when(f == pl.num_programs(2) - 1)
    def _():
        o_ref[...] = x_ref[...] + gate * acc_s[...]


def _ffn(x, mod, nw, w1, w3, w2, *, mod_row, tm, tf, side=()):
    nb, s, d = x.shape
    tm = min(tm, s)
    ni = s // tm
    nf = w1.shape[1] // tf
    side_in, side_out, side_shapes = _side_specs(side, nb * ni * nf, lambda b, i, f: (b * ni + i) * nf + f)
    return pl.pallas_call(
        functools.partial(_ffn_kernel, mod_row=mod_row, n_side=len(side)),
        grid=(nb, ni, nf),
        in_specs=[
            pl.BlockSpec((None, tm, d), lambda b, i, f: (b, i, 0)),
            pl.BlockSpec(mod.shape, lambda b, i, f: (0, 0)),
            pl.BlockSpec((1, d), lambda b, i, f: (0, 0)),
            pl.BlockSpec((d, tf), lambda b, i, f: (0, f)),
            pl.BlockSpec((d, tf), lambda b, i, f: (0, f)),
            pl.BlockSpec((tf, d), lambda b, i, f: (f, 0)),
            *side_in,
        ],
        out_specs=[pl.BlockSpec((None, tm, d), lambda b, i, f: (b, i, 0)), *side_out],
        out_shape=[jax.ShapeDtypeStruct(x.shape, F32), *side_shapes],
        scratch_shapes=[pltpu.VMEM((tm, d), BF16), pltpu.VMEM((tm, d), F32)],
        compiler_params=_params(("arbitrary", "arbitrary", "arbitrary")),
        name="ffn_swiglu",
    )(x, mod, nw, w1, w3, w2, *side)


def _qkv_kernel(x_ref, mod_ref, nw_ref, *rest, mod_row, j0):
    nj = 3 - j0
    w_refs, (gq_ref, gk_ref, o_ref) = rest[:nj], rest[nj:]
    tm, d = x_ref.shape
    row = pl.program_id(0) if mod_row is None else mod_row
    sh, sc, _ = _mod_slices(mod_ref, row, d, True)
    sub = min(SUB_ROWS, tm)

    def norm(k, _):
        return (_norm_mod(x_ref[pl.ds(k * sub, sub), :], nw_ref[...], sh, sc).astype(BF16),)

    def project(jj):
        def stage(k, state):
            return state[0], _dot(state[0], w_refs[jj][...])
        return stage

    def finish(jj):
        part = jj + j0

        def stage(k, state):
            h, y = state
            rows = pl.ds(k * sub, sub)
            if part == 2:
                o_ref[jj, rows, :] = y.astype(BF16)
                return (h,)
            g = gq_ref[...] if part == 0 else gk_ref[...]
            for hh in range(d // NA_HEAD_DIM):
                cols = slice(hh * NA_HEAD_DIM, (hh + 1) * NA_HEAD_DIM)
                blk = y[:, cols]
                ms = jnp.mean(blk * blk, axis=-1, keepdims=True)
                out = (blk * lax.rsqrt(ms + EPS)) * g
                if part == 0:
                    out = out * (NA_HEAD_DIM ** -0.5)
                o_ref[jj, rows, cols] = out.astype(BF16)
            return (h,)
        return stage

    stages = [norm]
    for jj in range(nj):
        stages += [project(jj), finish(jj)]
    _skewed(stages, tm // sub)


def _qkv(x, mod, nw, w_qkv, g_q, g_k, *, mod_row, tm, j0):
    nb, s, d = x.shape
    tm = min(tm, s)
    nj = 3 - j0
    const = lambda *shape: pl.BlockSpec(shape, lambda b, i: (0,) * len(shape))
    return pl.pallas_call(
        functools.partial(_qkv_kernel, mod_row=mod_row, j0=j0),
        grid=(nb, s // tm),
        in_specs=[
            pl.BlockSpec((None, tm, d), lambda b, i: (b, i, 0)),
            const(*mod.shape), const(1, d),
            *[pl.BlockSpec((d, d), lambda b, i, part=jj + j0: (0, part)) for jj in range(nj)],
            const(1, NA_HEAD_DIM), const(1, NA_HEAD_DIM),
        ],
        out_specs=pl.BlockSpec((nj, None, tm, d), lambda b, i: (0, b, i, 0)),
        out_shape=jax.ShapeDtypeStruct((nj, nb, s, d), BF16),
        compiler_params=_params(("arbitrary", "arbitrary")),
        name="qkv_proj",
    )(x, mod, nw, *([w_qkv] * nj), g_q, g_k)


def _bias_table(rpb):
    qcol = np.arange(GRID_W)[:, None]
    kcol = np.arange(GRID_W)[None, :]
    cs = np.clip(qcol - NA_COLS // 2, 0, GRID_W - NA_COLS)
    inwin = (kcol >= cs) & (kcol < cs + NA_COLS)
    dc = np.clip(kcol - qcol + NA_COLS - 1, 0, 2 * NA_COLS - 2)
    tab = jnp.where(inwin[None, None], rpb.astype(F32)[:, :, dc], NEG_INF)
    return jnp.concatenate([tab[:, :-1], tab[:, 1:]], axis=-1)


def _attn_kernel(q_ref, k_ref, v_ref, kc_ref, vc_ref, tab_ref, o_ref, s_s, p_s, *, kh):
    d = q_ref.shape[-1]
    heads = d // NA_HEAD_DIM
    rows = k_ref.shape[0] // GRID_W
    n_loc = kh * GRID_W
    r = pl.program_id(1)
    rs = jnp.clip(r - kh // 2, 0, rows - kh)
    dr0 = rs - r + (NA_ROWS - 1)
    start = pl.multiple_of(rs * GRID_W, GRID_W)
    win = pl.ds(start, n_loc)
    head_cols = [slice(hh * NA_HEAD_DIM, (hh + 1) * NA_HEAD_DIM) for hh in range(heads)]
    for hh, cols in enumerate(head_cols):
        qh = q_ref[:, cols]
        bias = jnp.concatenate([tab_ref[hh, pl.ds(dr0 + 2 * j, 1)][0] for j in range(kh // 2)], axis=-1)
        s_s[hh, :, :n_loc] = _dot_t(qh, k_ref[win, cols]) + bias
        s_s[hh, :, n_loc:] = _dot_t(qh, kc_ref[:, cols])
    dens = []
    for hh in range(heads):
        s = s_s[hh]
        p = jnp.exp(s - jnp.max(s, axis=-1, keepdims=True))
        dens.append(jnp.sum(p, axis=-1, keepdims=True))
        p_s[hh] = p.astype(BF16)
    for hh, cols in enumerate(head_cols):
        o = _dot(p_s[hh, :, :n_loc], v_ref[win, cols]) + _dot(p_s[hh, :, n_loc:], vc_ref[:, cols])
        o_ref[:, cols] = (o / dens[hh]).astype(BF16)


def _attention(qkv, kvc, tab):
    _, nb, s, d = qkv.shape
    n_ctx = kvc.shape[2] // nb
    rows = s // GRID_W
    kh = min(NA_ROWS, rows)
    once = pl.Buffered(1)
    return pl.pallas_call(
        functools.partial(_attn_kernel, kh=kh),
        grid=(nb, rows),
        in_specs=[
            pl.BlockSpec((None, None, GRID_W, d), lambda b, r: (0, b, r, 0)),
            pl.BlockSpec((None, None, s, d), lambda b, r: (1, b, 0, 0), pipeline_mode=once),
            pl.BlockSpec((None, None, s, d), lambda b, r: (2, b, 0, 0), pipeline_mode=once),
            pl.BlockSpec((None, None, n_ctx, d), lambda b, r: (0, 0, b, 0)),
            pl.BlockSpec((None, None, n_ctx, d), lambda b, r: (1, 0, b, 0)),
            pl.BlockSpec(tab.shape, lambda b, r: (0, 0, 0, 0), pipeline_mode=once),
        ],
        out_specs=pl.BlockSpec((None, GRID_W, d), lambda b, r: (b, r, 0)),
        out_shape=jax.ShapeDtypeStruct((nb, s, d), BF16),
        scratch_shapes=[pltpu.VMEM((d // NA_HEAD_DIM, GRID_W, kh * GRID_W + n_ctx), F32),
                        pltpu.VMEM((d // NA_HEAD_DIM, GRID_W, kh * GRID_W + n_ctx), BF16)],
        compiler_params=_params(("arbitrary", "arbitrary")),
        name="nbr_attention",
    )(qkv, qkv, qkv, kvc, kvc, tab)


def _proj_route_kernel(o_ref, x_ref, mod_ref, nw_ref, wo_ref, wrh_ref, wrl_ref, x2_ref, tok_ref, route_ref):
    tm, d = x_ref.shape
    row = pl.program_id(0)
    _, _, gate1 = _mod_slices(mod_ref, row, d, True)
    sh, sc, _ = _mod_slices(mod_ref, row, d, False)
    sub = min(SUB_ROWS, tm)
    lane = lax.broadcasted_iota(jnp.int32, (sub, LANE), 1)

    def project(k, _):
        rows = pl.ds(k * sub, sub)
        return _dot(o_ref[rows, :], wo_ref[...])

    def residual_norm(k, y):
        rows = pl.ds(k * sub, sub)
        x2 = x_ref[rows, :] + gate1 * y
        x2_ref[rows, :] = x2
        t = _norm_mod(x2, nw_ref[...], sh, sc)
        tok_ref[rows, :] = t
        return _split_bf16(t)

    def router(k, t_split):
        t_hi, t_lo = t_split
        return _dot(t_hi, wrh_ref[...]) + _dot(t_lo, wrh_ref[...]) + _dot(t_hi, wrl_ref[...])

    def top2(k, logits):
        logits = jnp.where(lane < MOE_EXPERTS, logits, -jnp.inf)
        m1 = jnp.max(logits, axis=-1, keepdims=True)
        i1 = jnp.min(jnp.where(logits == m1, lane, LANE), axis=-1, keepdims=True)
        rest = jnp.where(lane == i1, -jnp.inf, logits)
        m2 = jnp.max(rest, axis=-1, keepdims=True)
        i2 = jnp.min(jnp.where(rest == m2, lane, LANE), axis=-1, keepdims=True)
        e = jnp.exp(m2 - m1)
        den = 1.0 + e
        route_ref[pl.ds(k * sub, sub), :] = jnp.where(
            lane == 0, i1.astype(F32),
            jnp.where(lane == 1, i2.astype(F32),
                      jnp.where(lane == 2, 1.0 / den, jnp.where(lane == 3, e / den, 0.0))))

    _skewed((project, residual_norm, router, top2), tm // sub)


def _proj_route(o, x, mod, nw, w_o, wr_hi, wr_lo, *, tm):
    nb, s, d = x.shape
    const = lambda *shape: pl.BlockSpec(shape, lambda b, i: (0,) * len(shape))
    tile = lambda w: pl.BlockSpec((None, tm, w), lambda b, i: (b, i, 0))
    return pl.pallas_call(
        _proj_route_kernel,
        grid=(nb, s // tm),
        in_specs=[tile(d), tile(d), const(*mod.shape), const(1, d), const(d, d), const(d, LANE), const(d, LANE)],
        out_specs=[tile(d), tile(d), tile(LANE)],
        out_shape=[jax.ShapeDtypeStruct(x.shape, F32), jax.ShapeDtypeStruct(x.shape, F32),
                   jax.ShapeDtypeStruct((nb, s, LANE), F32)],
        compiler_params=_params(("arbitrary", "arbitrary")),
        name="attn_proj_route",
    )(o, x, mod, nw, w_o, wr_hi, wr_lo)


def _expert_kernel(be_ref, nu_ref, st_ref, stn_ref, tok_hbm, w1_ref, w3_ref, w2_ref, y_ref,
                   xf_s, xb_s, acc_s, sems, *, issue_rows):
    del be_ref
    b = pl.program_id(0)
    f = pl.program_id(1)
    bm = xb_s.shape[0]
    last = f == pl.num_programs(1) - 1
    n_used = nu_ref[0]
    used = b < n_used
    cur = b % 2

    def row_copy(tok, r, buf):
        return pltpu.make_async_copy(tok_hbm.at[pl.ds(tok, 1)], xf_s.at[buf, pl.ds(r, 1)], sems.at[buf])

    def request(idx_ref, buf, r0, n):
        def body(i, carry):
            for u in range(DMA_UNROLL):
                r = r0 + DMA_UNROLL * i + u
                row_copy(idx_ref[0, r], r, buf).start(priority=u % 2)
            return carry
        lax.fori_loop(0, n // DMA_UNROLL, body, 0)

    @pl.when(used)
    def _():
        @pl.when(jnp.logical_and(b == 0, f == 0))
        def _():
            request(st_ref, 0, 0, bm)

        @pl.when(jnp.logical_and(f < bm // issue_rows, b + 1 < n_used))
        def _():
            request(stn_ref, 1 - cur, f * issue_rows, issue_rows)

        @pl.when(f == 0)
        def _():
            def wait(i, carry):
                row_copy(0, 0, cur).wait()
                return carry
            lax.fori_loop(0, bm, wait, 0, unroll=8)
            xb_s[...] = xf_s[cur].astype(BF16)
            acc_s[...] = jnp.zeros_like(acc_s)

        xb = xb_s[...]
        hmid = (_silu(_dot(xb, w1_ref[...])) * _dot(xb, w3_ref[...])).astype(BF16)
        acc_s[...] += _dot(hmid, w2_ref[...])

        @pl.when(last)
        def _():
            y_ref[...] = acc_s[...]

    @pl.when(jnp.logical_and(jnp.logical_not(used), last))
    def _():
        y_ref[...] = jnp.zeros_like(y_ref)


def _experts(blk_e, n_used, slot_tok, toks, w1, w3, w2, *, bm, tf):
    n_blocks = slot_tok.shape[0]
    d = toks.shape[1]
    nf = w1.shape[2] // tf
    issue_steps = next(n for n in (8, 4, 2, 1) if n <= nf)
    issue_rows = bm // issue_steps
    assert issue_rows % DMA_UNROLL == 0

    def wcol(b, f, be, nu):
        return (be[b], 0, jnp.where(b < nu[0], f, nf - 1))

    def wrow(b, f, be, nu):
        return (be[b], jnp.where(b < nu[0], f, nf - 1), 0)

    return pl.pallas_call(
        functools.partial(_expert_kernel, issue_rows=issue_rows),
        grid_spec=pltpu.PrefetchScalarGridSpec(
            num_scalar_prefetch=2,
            grid=(n_blocks, nf),
            in_specs=[
                pl.BlockSpec((None, 1, bm), lambda b, f, be, nu: (b, 0, 0), memory_space=pltpu.SMEM),
                pl.BlockSpec((None, 1, bm), lambda b, f, be, nu: (jnp.minimum(b + 1, n_blocks - 1), 0, 0),
                             memory_space=pltpu.SMEM),
                pl.BlockSpec(memory_space=pl.ANY),
                pl.BlockSpec((None, d, tf), wcol),
                pl.BlockSpec((None, d, tf), wcol),
                pl.BlockSpec((None, tf, d), wrow),
            ],
            out_specs=pl.BlockSpec((bm, d), lambda b, f, be, nu: (b, 0)),
            scratch_shapes=[pltpu.VMEM((2, bm, d), F32), pltpu.VMEM((bm, d), BF16), pltpu.VMEM((bm, d), F32),
                            pltpu.SemaphoreType.DMA((2,))],
        ),
        out_shape=jax.ShapeDtypeStruct((n_blocks * bm, d), F32),
        compiler_params=_params(("arbitrary", "arbitrary")),
        name="moe_experts",
    )(blk_e, n_used, slot_tok, slot_tok, toks, w1, w3, w2)


def _combine_kernel(slot_ref, slotn_ref, route_ref, x_ref, mod_ref, y_hbm, o_ref, y_s, sems):
    rows, d = x_ref.shape
    step = pl.program_id(0) * pl.num_programs(1) + pl.program_id(1)
    n_steps = pl.num_programs(0) * pl.num_programs(1)
    cur = step % 2
    tok_unroll = DMA_UNROLL // MOE_TOP_K

    def row_copy(s, buf, k, r):
        return pltpu.make_async_copy(y_hbm.at[pl.ds(s, 1)], y_s.at[buf, k, pl.ds(r, 1)], sems.at[buf])

    def request(idx_ref, buf):
        def body(i, carry):
            for u in range(tok_unroll):
                r = tok_unroll * i + u
                for k in range(MOE_TOP_K):
                    row_copy(idx_ref[0, 0, MOE_TOP_K * r + k], buf, k, r).start(priority=k)
            return carry
        lax.fori_loop(0, rows // tok_unroll, body, 0)

    @pl.when(step == 0)
    def _():
        request(slot_ref, 0)

    @pl.when(step + 1 < n_steps)
    def _():
        request(slotn_ref, 1 - cur)

    def wait(i, carry):
        row_copy(0, cur, 0, 0).wait()
        return carry

    lax.fori_loop(0, rows * MOE_TOP_K, wait, 0, unroll=8)
    _, _, gate2 = _mod_slices(mod_ref, pl.program_id(0), d, False)
    route = route_ref[...]
    f = route[:, 2:3] * y_s[cur, 0] + route[:, 3:4] * y_s[cur, 1]
    o_ref[...] = x_ref[...] + gate2 * f


def _combine(slot, route, x, mod, y, *, rows):
    nb, s, d = x.shape
    steps = s // rows
    idx_spec = lambda ahead: pl.BlockSpec(
        (1, 1, MOE_TOP_K * rows), lambda b, i: (jnp.minimum(b * steps + i + ahead, nb * steps - 1), 0, 0),
        memory_space=pltpu.SMEM)
    slot = slot.reshape(nb * steps, 1, MOE_TOP_K * rows)
    return pl.pallas_call(
        _combine_kernel,
        grid=(nb, steps),
        in_specs=[
            idx_spec(0), idx_spec(1),
            pl.BlockSpec((None, rows, LANE), lambda b, i: (b, i, 0)),
            pl.BlockSpec((None, rows, d), lambda b, i: (b, i, 0)),
            pl.BlockSpec(mod.shape, lambda b, i: (0, 0)),
            pl.BlockSpec(memory_space=pl.ANY),
        ],
        out_specs=pl.BlockSpec((None, rows, d), lambda b, i: (b, i, 0)),
        out_shape=jax.ShapeDtypeStruct(x.shape, F32),
        scratch_shapes=[pltpu.VMEM((2, MOE_TOP_K, rows, d), F32), pltpu.SemaphoreType.DMA((2,))],
        compiler_params=_params(("arbitrary", "arbitrary")),
        name="moe_combine",
    )(slot, slot, route, x, mod, y)


def _moe_plan(route, bm):
    n_tok = route.shape[0]
    e_flat = route[:, :MOE_TOP_K].astype(jnp.int32).reshape(-1)
    onehot = (e_flat[:, None] == jnp.arange(MOE_EXPERTS)[None, :]).astype(jnp.int32)
    csum = jnp.cumsum(onehot, axis=0)
    counts = csum[-1]
    rank = jnp.sum((csum - onehot) * onehot, axis=1)
    padded = (counts + bm - 1) // bm * bm
    pad_end = jnp.cumsum(padded)
    slot = (pad_end - padded)[e_flat] + rank
    n_blocks = (n_tok * MOE_TOP_K) // bm + MOE_EXPERTS
    blk_e = jnp.minimum(jnp.searchsorted(pad_end, jnp.arange(n_blocks) * bm, side="right"), MOE_EXPERTS - 1)
    slot = slot.astype(jnp.int32)
    slot_tok = jnp.zeros((n_blocks * bm,), jnp.int32).at[slot].set(
        jnp.arange(n_tok * MOE_TOP_K, dtype=jnp.int32) // MOE_TOP_K, unique_indices=True)
    return (slot, slot_tok.reshape(n_blocks, 1, bm), blk_e.astype(jnp.int32),
            (pad_end[-1:] // bm).astype(jnp.int32))


def kernel(x, c, ctx, c_ctx, w_ada, b_ada, norm_w, mlp_w_in, mlp_g_v, mlp_w_s, mlp_b_s, mlp_w_out,
           ffn_w1, ffn_w3, ffn_w2, na_w_qkv, na_g_q, na_g_k, na_rpb, na_w_o,
           moe_w_router, moe_w1, moe_w3, moe_w2):
    nb, s, d = x.shape
    n_ctx = ctx.shape[1]
    assert nb + 1 <= MOD_ROWS and w_ada.shape[0] == 2
    inner = mlp_w_out.shape[1]
    assert inner // MLP_GROUPS == LANE and d % NA_HEAD_DIM == 0

    cc = jnp.zeros((MOD_ROWS, d), F32).at[:nb].set(c).at[nb].set(c_ctx)
    mod = _ada_mod(cc, w_ada, b_ada)
    ctx_flat = ctx.reshape(1, nb * n_ctx, d)

    nw0 = norm_w[0]
    w_in = mlp_w_in[0].astype(BF16)
    w_s = mlp_w_s[0].astype(BF16)
    w_out = mlp_w_out[0].astype(BF16)
    g_v = mlp_g_v[0][None]
    b_full = jnp.repeat(mlp_b_s[0].T, inner // MLP_GROUPS, axis=1)
    mix = functools.partial(_mixer, mod=mod[0], nw=nw0[0][None], w_in=w_in, g_v=g_v, w_s=w_s, b_full=b_full,
                            w_out=w_out, tm=256)
    n_exp, _, moe_dim = moe_w1.shape[1:]
    x1, e_w2 = mix(x, mod_row=None, side=(moe_w2[0].reshape(n_exp * moe_dim, d),))
    xc1, = mix(ctx_flat, mod_row=nb)
    ffn = functools.partial(_ffn, mod=mod[0], nw=nw0[1][None], w1=ffn_w1[0].astype(BF16),
                            w3=ffn_w3[0].astype(BF16), w2=ffn_w2[0].astype(BF16), tm=512, tf=512)
    x1, e_w1, e_w3 = ffn(x1, mod_row=None, side=(moe_w1[0].reshape(n_exp * d, moe_dim),
                                                  moe_w3[0].reshape(n_exp * d, moe_dim)))
    xc1, = ffn(xc1, mod_row=nb)

    nw1 = norm_w[1]
    w_qkv = na_w_qkv[0].astype(BF16)
    qkv_fn = functools.partial(_qkv, mod=mod[1], nw=nw1[0][None], w_qkv=w_qkv, g_q=na_g_q[0][None],
                               g_k=na_g_k[0][None], tm=512)
    qkv = qkv_fn(x1, mod_row=None, j0=0)
    kvc = qkv_fn(xc1, mod_row=nb, j0=1)
    o = _attention(qkv, kvc, _bias_table(na_rpb[0]))

    wr = jnp.zeros((d, LANE), F32).at[:, :MOE_EXPERTS].set(moe_w_router[0])
    wr_hi, wr_lo = _split_bf16(wr)
    x2, toks, route = _proj_route(o, x1, mod[1], nw1[1][None], na_w_o[0].astype(BF16), wr_hi, wr_lo, tm=512)

    bm = 512
    n_tok = nb * s
    route_flat = route.reshape(n_tok, LANE)
    slot, slot_tok, blk_e, n_used = _moe_plan(route_flat, bm)
    ys = _experts(blk_e, n_used, slot_tok, toks.reshape(n_tok, d), e_w1.reshape(n_exp, d, moe_dim),
                  e_w3.reshape(n_exp, d, moe_dim), e_w2.reshape(n_exp, moe_dim, d), bm=bm, tf=min(1024, moe_dim))
    return _combine(slot, route, x2, mod[1], ys, rows=256)
```

```python
import functools

import numpy as np
import jax
import jax.numpy as jnp
from jax import lax
from jax.experimental import pallas as pl
from jax.experimental.pallas import tpu as pltpu

GRID_W = 64
N_MOD = 6
MLP_CHUNK = 128
MLP_GROUPS = 16
NA_HEAD_DIM = 128
NA_ROWS = 8
NA_COLS = 16
MOE_EXPERTS = 8
MOE_TOP_K = 2
EPS = 1e-6
NEG_INF = -1e30

LANE = 128
SUB_ROWS = 128
ROW_GROUP = 8
MOD_ROWS = 16
VMEM_LIMIT = 56 * 1024 * 1024

F32 = jnp.float32
BF16 = jnp.bfloat16


def _dot(a, b):
    return jnp.dot(a, b, preferred_element_type=F32)


def _dot_t(a, b):
    return lax.dot_general(a, b, (((1,), (1,)), ((), ())), preferred_element_type=F32)


def _split_bf16(a):
    hi = a.astype(BF16)
    lo = (a - hi.astype(F32)).astype(BF16)
    return hi, lo


def _silu(a):
    return a * jax.nn.sigmoid(a)


def _norm_mod(x, nw, shift, scale):
    ms = jnp.mean(x * x, axis=-1, keepdims=True)
    y = x * lax.rsqrt(ms + EPS)
    return (y * nw) * (1.0 + scale) + shift


def _mod_slices(mod_ref, row, d, first):
    base = 0 if first else 3
    return tuple(mod_ref[pl.ds(row, 1), (base + k) * d:(base + k + 1) * d] for k in range(3))


def _skewed(stages, n):
    state = [None] * n
    for t in range(n + len(stages) - 1):
        for s in reversed(range(len(stages))):
            k = t - s
            if 0 <= k < n:
                state[k] = stages[s](k, state[k])


def _params(sem):
    return pltpu.CompilerParams(dimension_semantics=sem, vmem_limit_bytes=VMEM_LIMIT)


def _side_rows(rows, n_steps):
    return next(r for r in range(16, rows + 1, 16) if rows % r == 0 and rows // r <= n_steps)


def _side_specs(side, n_steps, step_of):
    in_specs, out_specs, out_shapes = [], [], []
    for w in side:
        rows, cols = w.shape
        r = _side_rows(rows, n_steps)
        index = lambda *ids, last=rows // r - 1: (jnp.minimum(step_of(*ids), last), 0)
        in_specs.append(pl.BlockSpec((r, cols), index))
        out_specs.append(pl.BlockSpec((r, cols), index))
        out_shapes.append(jax.ShapeDtypeStruct(w.shape, BF16))
    return in_specs, out_specs, out_shapes


def _side_cast(side_in, side_out):
    for src, dst in zip(side_in, side_out):
        dst[...] = src[...].astype(BF16)


def _ada_kernel(a_ref, w_ref, b_ref, o_ref):
    a_hi, a_lo = _split_bf16(_silu(a_ref[...]))
    w_hi, w_lo = _split_bf16(w_ref[...])
    o_ref[...] = _dot(a_hi, w_hi) + _dot(a_lo, w_hi) + _dot(a_hi, w_lo) + b_ref[...]


def _ada_mod(cc, w_ada, b_ada, tn=1024):
    nl, d, n = w_ada.shape
    tn = min(tn, d)
    return pl.pallas_call(
        _ada_kernel,
        grid=(nl, n // tn),
        in_specs=[
            pl.BlockSpec((MOD_ROWS, d), lambda l, j: (0, 0)),
            pl.BlockSpec((None, d, tn), lambda l, j: (l, 0, j)),
            pl.BlockSpec((None, 1, tn), lambda l, j: (l, 0, j)),
        ],
        out_specs=pl.BlockSpec((None, MOD_ROWS, tn), lambda l, j: (l, 0, j)),
        out_shape=jax.ShapeDtypeStruct((nl, MOD_ROWS, n), F32),
        compiler_params=_params(("arbitrary", "arbitrary")),
        name="ada_mod",
    )(cc, w_ada, b_ada.reshape(nl, 1, n))


def _mixer_kernel(x_ref, mod_ref, nw_ref, win_ref, gv_ref, ws_ref, bs_ref, wout_ref, *rest,
                  mod_row, col_chunk, n_side):
    side_in, (o_ref, *side_out), (u_s, v_s, m_s) = rest[:n_side], rest[n_side:2 * n_side + 1], rest[2 * n_side + 1:]
    _side_cast(side_in, side_out)
    tm, d = x_ref.shape
    inner = u_s.shape[1]
    gd = inner // MLP_GROUPS
    row = pl.program_id(0) if mod_row is None else mod_row
    sh, sc, gate = _mod_slices(mod_ref, row, d, True)
    x = x_ref[...]
    h = _norm_mod(x, nw_ref[...], sh, sc).astype(BF16)
    ss = jnp.zeros((tm, 1), F32)
    for c in range(2 * inner // col_chunk):
        lo = c * col_chunk
        z = _dot(h, win_ref[:, lo:lo + col_chunk])
        z = 0.5 * z * (1.0 + lax.erf(z * (2.0 ** -0.5)))
        if lo < inner:
            u_s[:, lo:lo + col_chunk] = z
        else:
            v_s[:, lo - inner:lo - inner + col_chunk] = z
            ss = ss + jnp.sum(z * z, axis=-1, keepdims=True)
    rinv = lax.rsqrt(ss / inner + EPS)
    for n in range(tm // MLP_CHUNK):
        rows = slice(n * MLP_CHUNK, (n + 1) * MLP_CHUNK)
        for g in range(MLP_GROUPS):
            cols = slice(g * gd, (g + 1) * gd)
            vb = ((v_s[rows, cols] * rinv[rows]) * gv_ref[:, cols]).astype(BF16)
            mixed = _dot(ws_ref[g], vb) + bs_ref[:, cols]
            m_s[rows, cols] = (u_s[rows, cols] * mixed).astype(BF16)
    y = _dot(m_s[...], wout_ref[...])
    o_ref[...] = x + gate * y


def _mixer(x, mod, nw, w_in, g_v, w_s, b_full, w_out, *, mod_row, tm, side=()):
    nb, s, d = x.shape
    inner = w_out.shape[0]
    tm = min(tm, s)
    ni = s // tm
    const = lambda *shape: pl.BlockSpec(shape, lambda b, i: (0,) * len(shape))
    side_in, side_out, side_shapes = _side_specs(side, nb * ni, lambda b, i: b * ni + i)
    return pl.pallas_call(
        functools.partial(_mixer_kernel, mod_row=mod_row, col_chunk=min(512, inner), n_side=len(side)),
        grid=(nb, ni),
        in_specs=[
            pl.BlockSpec((None, tm, d), lambda b, i: (b, i, 0)),
            const(*mod.shape), const(1, d), const(d, 2 * inner), const(1, inner),
            const(*w_s.shape), const(MLP_CHUNK, inner), const(inner, d), *side_in,
        ],
        out_specs=[pl.BlockSpec((None, tm, d), lambda b, i: (b, i, 0)), *side_out],
        out_shape=[jax.ShapeDtypeStruct(x.shape, F32), *side_shapes],
        scratch_shapes=[pltpu.VMEM((tm, inner), F32), pltpu.VMEM((tm, inner), F32),
                        pltpu.VMEM((tm, inner), BF16)],
        compiler_params=_params(("arbitrary", "arbitrary")),
        name="gmlp_mixer",
    )(x, mod, nw, w_in, g_v, w_s, b_full, w_out, *side)


def _ffn_kernel(x_ref, mod_ref, nw_ref, w1_ref, w3_ref, w2_ref, *rest, mod_row, n_side):
    side_in, (o_ref, *side_out), (t_s, acc_s) = rest[:n_side], rest[n_side:2 * n_side + 1], rest[2 * n_side + 1:]
    _side_cast(side_in, side_out)
    d = x_ref.shape[-1]
    f = pl.program_id(2)
    row = pl.program_id(0) if mod_row is None else mod_row
    sh, sc, gate = _mod_slices(mod_ref, row, d, False)

    def up(t):
        return (_silu(_dot(t, w1_ref[...])) * _dot(t, w3_ref[...])).astype(BF16)

    @pl.when(f == 0)
    def _():
        sub = min(SUB_ROWS, x_ref.shape[0])

        def norm(k, _):
            rows = pl.ds(k * sub, sub)
            t = _norm_mod(x_ref[rows, :], nw_ref[...], sh, sc).astype(BF16)
            t_s[rows, :] = t
            return t

        def down(k, hmid):
            acc_s[pl.ds(k * sub, sub), :] = _dot(hmid, w2_ref[...])

        _skewed((norm, lambda k, t: up(t), down), x_ref.shape[0] // sub)

    @pl.when(f > 0)
    def _():
        acc_s[...] += _dot(up(t_s[...]), w2_ref[...])

    @pl.when(f == pl.num_programs(2) - 1)
    def _():
        o_ref[...] = x_ref[...] + gate * acc_s[...]


def _ffn(x, mod, nw, w1, w3, w2, *, mod_row, tm, tf, side=()):
    nb, s, d = x.shape
    tm = min(tm, s)
    ni = s // tm
    nf = w1.shape[1] // tf
    side_in, side_out, side_shapes = _side_specs(side, nb * ni * nf, lambda b, i, f: (b * ni + i) * nf + f)
    return pl.pallas_call(
        functools.partial(_ffn_kernel, mod_row=mod_row, n_side=len(side)),
        grid=(nb, ni, nf),
        in_specs=[
            pl.BlockSpec((None, tm, d), lambda b, i, f: (b, i, 0)),
            pl.BlockSpec(mod.shape, lambda b, i, f: (0, 0)),
            pl.BlockSpec((1, d), lambda b, i, f: (0, 0)),
            pl.BlockSpec((d, tf), lambda b, i, f: (0, f)),
            pl.BlockSpec((d, tf), lambda b, i, f: (0, f)),
            pl.BlockSpec((tf, d), lambda b, i, f: (f, 0)),
            *side_in,
        ],
        out_specs=[pl.BlockSpec((None, tm, d), lambda b, i, f: (b, i, 0)), *side_out],
        out_shape=[jax.ShapeDtypeStruct(x.shape, F32), *side_shapes],
        scratch_shapes=[pltpu.VMEM((tm, d), BF16), pltpu.VMEM((tm, d), F32)],
        compiler_params=_params(("arbitrary", "arbitrary", "arbitrary")),
        name="ffn_swiglu",
    )(x, mod, nw, w1, w3, w2, *side)


def _qkv_kernel(x_ref, mod_ref, nw_ref, *rest, mod_row, j0):
    nj = 3 - j0
    w_refs, (gq_ref, gk_ref, o_ref) = rest[:nj], rest[nj:]
    tm, d = x_ref.shape
    row = pl.program_id(0) if mod_row is None else mod_row
    sh, sc, _ = _mod_slices(mod_ref, row, d, True)
    sub = min(SUB_ROWS, tm)

    def norm(k, _):
        return (_norm_mod(x_ref[pl.ds(k * sub, sub), :], nw_ref[...], sh, sc).astype(BF16),)

    def project(jj):
        def stage(k, state):
            return state[0], _dot(state[0], w_refs[jj][...])
        return stage

    def finish(jj):
        part = jj + j0

        def stage(k, state):
            h, y = state
            rows = pl.ds(k * sub, sub)
            if part == 2:
                o_ref[jj, rows, :] = y.astype(BF16)
                return (h,)
            g = gq_ref[...] if part == 0 else gk_ref[...]
            for hh in range(d // NA_HEAD_DIM):
                cols = slice(hh * NA_HEAD_DIM, (hh + 1) * NA_HEAD_DIM)
                blk = y[:, cols]
                ms = jnp.mean(blk * blk, axis=-1, keepdims=True)
                out = (blk * lax.rsqrt(ms + EPS)) * g
                if part == 0:
                    out = out * (NA_HEAD_DIM ** -0.5)
                o_ref[jj, rows, cols] = out.astype(BF16)
            return (h,)
        return stage

    stages = [norm]
    for jj in range(nj):
        stages += [project(jj), finish(jj)]
    _skewed(stages, tm // sub)


def _qkv(x, mod, nw, w_qkv, g_q, g_k, *, mod_row, tm, j0):
    nb, s, d = x.shape
    tm = min(tm, s)
    nj = 3 - j0
    const = lambda *shape: pl.BlockSpec(shape, lambda b, i: (0,) * len(shape))
    return pl.pallas_call(
        functools.partial(_qkv_kernel, mod_row=mod_row, j0=j0),
        grid=(nb, s // tm),
        in_specs=[
            pl.BlockSpec((None, tm, d), lambda b, i: (b, i, 0)),
            const(*mod.shape), const(1, d),
            *[pl.BlockSpec((d, d), lambda b, i, part=jj + j0: (0, part)) for jj in range(nj)],
            const(1, NA_HEAD_DIM), const(1, NA_HEAD_DIM),
        ],
        out_specs=pl.BlockSpec((nj, None, tm, d), lambda b, i: (0, b, i, 0)),
        out_shape=jax.ShapeDtypeStruct((nj, nb, s, d), BF16),
        compiler_params=_params(("arbitrary", "arbitrary")),
        name="qkv_proj",
    )(x, mod, nw, *([w_qkv] * nj), g_q, g_k)


def _bias_table(rpb):
    qcol = np.arange(GRID_W)[:, None]
    kcol = np.arange(GRID_W)[None, :]
    cs = np.clip(qcol - NA_COLS // 2, 0, GRID_W - NA_COLS)
    inwin = (kcol >= cs) & (kcol < cs + NA_COLS)
    dc = np.clip(kcol - qcol + NA_COLS - 1, 0, 2 * NA_COLS - 2)
    tab = jnp.where(inwin[None, None], rpb.astype(F32)[:, :, dc], NEG_INF)
    return jnp.concatenate([tab[:, :-1], tab[:, 1:]], axis=-1)


def _attn_kernel(q_ref, k_ref, v_ref, kc_ref, vc_ref, tab_ref, o_ref, s_s, p_s, *, kh):
    d = q_ref.shape[-1]
    heads = d // NA_HEAD_DIM
    rows = k_ref.shape[0] // GRID_W
    n_loc = kh * GRID_W
    rows_per_step = q_ref.shape[0] // GRID_W
    units = []
    for rr in range(rows_per_step):
        r = pl.program_id(1) * rows_per_step + rr
        rs = jnp.clip(r - kh // 2, 0, rows - kh)
        dr0 = rs - r + (NA_ROWS - 1)
        win = pl.ds(pl.multiple_of(rs * GRID_W, GRID_W), n_loc)
        qrows = slice(rr * GRID_W, (rr + 1) * GRID_W)
        for hh in range(heads):
            units.append((rr * heads + hh, hh, qrows, slice(hh * NA_HEAD_DIM, (hh + 1) * NA_HEAD_DIM), dr0, win))
    for u, hh, qrows, cols, dr0, win in units:
        qh = q_ref[qrows, cols]
        bias = jnp.concatenate([tab_ref[hh, pl.ds(dr0 + 2 * j, 1)][0] for j in range(kh // 2)], axis=-1)
        s_s[u, :, :n_loc] = _dot_t(qh, k_ref[win, cols]) + bias
        s_s[u, :, n_loc:] = _dot_t(qh, kc_ref[:, cols])
    dens = []
    for u in range(len(units)):
        s = s_s[u]
        p = jnp.exp(s - jnp.max(s, axis=-1, keepdims=True))
        dens.append(jnp.sum(p, axis=-1, keepdims=True))
        p_s[u] = p.astype(BF16)
    for u, hh, qrows, cols, dr0, win in units:
        o = _dot(p_s[u, :, :n_loc], v_ref[win, cols]) + _dot(p_s[u, :, n_loc:], vc_ref[:, cols])
        o_ref[qrows, cols] = (o / dens[u]).astype(BF16)


def _attention(qkv, kvc, tab):
    _, nb, s, d = qkv.shape
    n_ctx = kvc.shape[2] // nb
    rows = s // GRID_W
    kh = min(NA_ROWS, rows)
    once = pl.Buffered(1)
    rps = 2 if rows % 2 == 0 else 1
    n_units = rps * (d // NA_HEAD_DIM)
    return pl.pallas_call(
        functools.partial(_attn_kernel, kh=kh),
        grid=(nb, rows // rps),
        in_specs=[
            pl.BlockSpec((None, None, rps * GRID_W, d), lambda b, r: (0, b, r, 0)),
            pl.BlockSpec((None, None, s, d), lambda b, r: (1, b, 0, 0), pipeline_mode=once),
            pl.BlockSpec((None, None, s, d), lambda b, r: (2, b, 0, 0), pipeline_mode=once),
            pl.BlockSpec((None, None, n_ctx, d), lambda b, r: (0, 0, b, 0)),
            pl.BlockSpec((None, None, n_ctx, d), lambda b, r: (1, 0, b, 0)),
            pl.BlockSpec(tab.shape, lambda b, r: (0, 0, 0, 0), pipeline_mode=once),
        ],
        out_specs=pl.BlockSpec((None, rps * GRID_W, d), lambda b, r: (b, r, 0)),
        out_shape=jax.ShapeDtypeStruct((nb, s, d), BF16),
        scratch_shapes=[pltpu.VMEM((n_units, GRID_W, kh * GRID_W + n_ctx), F32),
                        pltpu.VMEM((n_units, GRID_W, kh * GRID_W + n_ctx), BF16)],
        compiler_params=_params(("arbitrary", "arbitrary")),
        name="nbr_attention",
    )(qkv, qkv, qkv, kvc, kvc, tab)


def _proj_route_kernel(o_ref, x_ref, mod_ref, nw_ref, wo_ref, wrh_ref, wrl_ref, x2_ref, tok_ref, route_ref):
    tm, d = x_ref.shape
    row = pl.program_id(0)
    _, _, gate1 = _mod_slices(mod_ref, row, d, True)
    sh, sc, _ = _mod_slices(mod_ref, row, d, False)
    sub = min(SUB_ROWS, tm)
    lane = lax.broadcasted_iota(jnp.int32, (sub, LANE), 1)

    def project(k, _):
        rows = pl.ds(k * sub, sub)
        return _dot(o_ref[rows, :], wo_ref[...])

    def residual_norm(k, y):
        rows = pl.ds(k * sub, sub)
        x2 = x_ref[rows, :] + gate1 * y
        x2_ref[rows, :] = x2
        t = _norm_mod(x2, nw_ref[...], sh, sc)
        tok_ref[rows, :] = t
        return _split_bf16(t)

    def router(k, t_split):
        t_hi, t_lo = t_split
        return _dot(t_hi, wrh_ref[...]) + _dot(t_lo, wrh_ref[...]) + _dot(t_hi, wrl_ref[...])

    def top2(k, logits):
        logits = jnp.where(lane < MOE_EXPERTS, logits, -jnp.inf)
        m1 = jnp.max(logits, axis=-1, keepdims=True)
        i1 = jnp.min(jnp.where(logits == m1, lane, LANE), axis=-1, keepdims=True)
        rest = jnp.where(lane == i1, -jnp.inf, logits)
        m2 = jnp.max(rest, axis=-1, keepdims=True)
        i2 = jnp.min(jnp.where(rest == m2, lane, LANE), axis=-1, keepdims=True)
        e = jnp.exp(m2 - m1)
        den = 1.0 + e
        route_ref[pl.ds(k * sub, sub), :] = jnp.where(
            lane == 0, i1.astype(F32),
            jnp.where(lane == 1, i2.astype(F32),
                      jnp.where(lane == 2, 1.0 / den, jnp.where(lane == 3, e / den, 0.0))))

    _skewed((project, residual_norm, router, top2), tm // sub)


def _proj_route(o, x, mod, nw, w_o, wr_hi, wr_lo, *, tm):
    nb, s, d = x.shape
    const = lambda *shape: pl.BlockSpec(shape, lambda b, i: (0,) * len(shape))
    tile = lambda w: pl.BlockSpec((None, tm, w), lambda b, i: (b, i, 0))
    return pl.pallas_call(
        _proj_route_kernel,
        grid=(nb, s // tm),
        in_specs=[tile(d), tile(d), const(*mod.shape), const(1, d), const(d, d), const(d, LANE), const(d, LANE)],
        out_specs=[tile(d), tile(d), tile(LANE)],
        out_shape=[jax.ShapeDtypeStruct(x.shape, F32), jax.ShapeDtypeStruct(x.shape, F32),
                   jax.ShapeDtypeStruct((nb, s, LANE), F32)],
        compiler_params=_params(("arbitrary", "arbitrary")),
        name="attn_proj_route",
    )(o, x, mod, nw, w_o, wr_hi, wr_lo)


def _expert_kernel(be_ref, nu_ref, st_ref, stn_ref, tok_hbm, w1_ref, w3_ref, w2_ref, y_ref,
                   xf_s, xb_s, acc_s, sems, *, groups_per_step):
    del be_ref
    b = pl.program_id(0)
    f = pl.program_id(1)
    bm = xb_s.shape[0]
    n_groups = xf_s.shape[1]
    last = f == pl.num_programs(1) - 1
    n_used = nu_ref[0]
    used = b < n_used
    cur = b % 2

    def row_copy(tok, grp, u, buf):
        return pltpu.make_async_copy(tok_hbm.at[pl.ds(tok, 1)], xf_s.at[buf, grp, pl.ds(u, 1)], sems.at[buf])

    def request_group(idx_ref, grp, buf):
        for u in range(ROW_GROUP):
            row_copy(idx_ref[0, grp * ROW_GROUP + u], grp, u, buf).start(priority=u % 2)

    def wait_all(buf):
        def wait(i, carry):
            row_copy(0, 0, 0, buf).wait()
            return carry
        lax.fori_loop(0, n_groups * ROW_GROUP, wait, 0, unroll=8)

    @pl.when(used)
    def _():
        @pl.when(jnp.logical_and(b == 0, f == 0))
        def _():
            def body(grp, carry):
                request_group(st_ref, grp, 0)
                return carry
            lax.fori_loop(0, n_groups, body, 0)

        @pl.when(f == 0)
        def _():
            wait_all(cur)
            xb_s[...] = xf_s[cur, :bm // ROW_GROUP].reshape(xb_s.shape).astype(BF16)
            acc_s[...] = jnp.zeros_like(acc_s)

        xb = xb_s[...]
        hmid = (_silu(_dot(xb, w1_ref[...])) * _dot(xb, w3_ref[...])).astype(BF16)
        acc_s[...] += _dot(hmid, w2_ref[...])

        for j in range(groups_per_step):
            request_group(stn_ref, f * groups_per_step + j, 1 - cur)

        @pl.when(last)
        def _():
            y_ref[...] = acc_s[...]

        @pl.when(jnp.logical_and(last, b == n_used - 1))
        def _():
            wait_all(1 - cur)

    @pl.when(jnp.logical_and(jnp.logical_not(used), last))
    def _():
        y_ref[...] = jnp.zeros_like(y_ref)


def _experts(blk_e, n_used, slot_tok, toks, w1, w3, w2, *, bm, tf):
    n_blocks = slot_tok.shape[0]
    d = toks.shape[1]
    nf = w1.shape[2] // tf
    groups_per_step = -(-bm // (ROW_GROUP * nf))
    n_groups = groups_per_step * nf
    slot_tok = jnp.pad(slot_tok, ((0, 0), (0, n_groups * ROW_GROUP - bm))).reshape(n_blocks, 1, n_groups * ROW_GROUP)

    def wcol(b, f, be, nu):
        return (be[b], 0, jnp.where(b < nu[0], f, nf - 1))

    def wrow(b, f, be, nu):
        return (be[b], jnp.where(b < nu[0], f, nf - 1), 0)

    return pl.pallas_call(
        functools.partial(_expert_kernel, groups_per_step=groups_per_step),
        grid_spec=pltpu.PrefetchScalarGridSpec(
            num_scalar_prefetch=2,
            grid=(n_blocks, nf),
            in_specs=[
                pl.BlockSpec((None, 1, n_groups * ROW_GROUP), lambda b, f, be, nu: (b, 0, 0),
                             memory_space=pltpu.SMEM),
                pl.BlockSpec((None, 1, n_groups * ROW_GROUP),
                             lambda b, f, be, nu: (jnp.minimum(b + 1, n_blocks - 1), 0, 0),
                             memory_space=pltpu.SMEM),
                pl.BlockSpec(memory_space=pl.ANY),
                pl.BlockSpec((None, d, tf), wcol),
                pl.BlockSpec((None, d, tf), wcol),
                pl.BlockSpec((None, tf, d), wrow),
            ],
            out_specs=pl.BlockSpec((bm, d), lambda b, f, be, nu: (b, 0)),
            scratch_shapes=[pltpu.VMEM((2, n_groups, ROW_GROUP, d), F32), pltpu.VMEM((bm, d), BF16),
                            pltpu.VMEM((bm, d), F32),
                            pltpu.SemaphoreType.DMA((2,))],
        ),
        out_shape=jax.ShapeDtypeStruct((n_blocks * bm, d), F32),
        compiler_params=_params(("arbitrary", "arbitrary")),
        name="moe_experts",
    )(blk_e, n_used, slot_tok, slot_tok, toks, w1, w3, w2)


def _combine_kernel(slot_ref, slotn_ref, route_ref, x_ref, mod_ref, y_hbm, o_ref, y_s, sems):
    rows, d = x_ref.shape
    step = pl.program_id(0) * pl.num_programs(1) + pl.program_id(1)
    n_steps = pl.num_programs(0) * pl.num_programs(1)
    cur = step % 2

    def row_copy(s, buf, k, grp, u):
        return pltpu.make_async_copy(y_hbm.at[pl.ds(s, 1)], y_s.at[buf, k, grp, pl.ds(u, 1)], sems.at[buf])

    def request(idx_ref, buf):
        def body(grp, carry):
            for u in range(ROW_GROUP):
                for k in range(MOE_TOP_K):
                    row_copy(idx_ref[0, 0, MOE_TOP_K * (grp * ROW_GROUP + u) + k], buf, k, grp, u).start(priority=k)
            return carry
        lax.fori_loop(0, rows // ROW_GROUP, body, 0)

    @pl.when(step == 0)
    def _():
        request(slot_ref, 0)

    @pl.when(step + 1 < n_steps)
    def _():
        request(slotn_ref, 1 - cur)

    def wait(i, carry):
        row_copy(0, cur, 0, 0, 0).wait()
        return carry

    lax.fori_loop(0, rows * MOE_TOP_K, wait, 0, unroll=8)
    _, _, gate2 = _mod_slices(mod_ref, pl.program_id(0), d, False)
    route = route_ref[...]
    f = route[:, 2:3] * y_s[cur, 0].reshape(rows, d) + route[:, 3:4] * y_s[cur, 1].reshape(rows, d)
    o_ref[...] = x_ref[...] + gate2 * f


def _combine(slot, route, x, mod, y, *, rows):
    nb, s, d = x.shape
    steps = s // rows
    idx_spec = lambda ahead: pl.BlockSpec(
        (1, 1, MOE_TOP_K * rows), lambda b, i: (jnp.minimum(b * steps + i + ahead, nb * steps - 1), 0, 0),
        memory_space=pltpu.SMEM)
    slot = slot.reshape(nb * steps, 1, MOE_TOP_K * rows)
    return pl.pallas_call(
        _combine_kernel,
        grid=(nb, steps),
        in_specs=[
            idx_spec(0), idx_spec(1),
            pl.BlockSpec((None, rows, LANE), lambda b, i: (b, i, 0)),
            pl.BlockSpec((None, rows, d), lambda b, i: (b, i, 0)),
            pl.BlockSpec(mod.shape, lambda b, i: (0, 0)),
            pl.BlockSpec(memory_space=pl.ANY),
        ],
        out_specs=pl.BlockSpec((None, rows, d), lambda b, i: (b, i, 0)),
        out_shape=jax.ShapeDtypeStruct(x.shape, F32),
        scratch_shapes=[pltpu.VMEM((2, MOE_TOP_K, rows // ROW_GROUP, ROW_GROUP, d), F32),
                        pltpu.SemaphoreType.DMA((2,))],
        compiler_params=_params(("arbitrary", "arbitrary")),
        name="moe_combine",
    )(slot, slot, route, x, mod, y)


def _moe_plan(route, bm):
    n_tok = route.shape[0]
    e_flat = route[:, :MOE_TOP_K].astype(jnp.int32).reshape(-1)
    onehot = (e_flat[:, None] == jnp.arange(MOE_EXPERTS)[None, :]).astype(jnp.int32)
    csum = jnp.cumsum(onehot, axis=0)
    counts = csum[-1]
    rank = jnp.sum((csum - onehot) * onehot, axis=1)
    padded = (counts + bm - 1) // bm * bm
    pad_end = jnp.cumsum(padded)
    slot = (pad_end - padded)[e_flat] + rank
    n_blocks = (n_tok * MOE_TOP_K) // bm + MOE_EXPERTS
    blk_e = jnp.minimum(jnp.searchsorted(pad_end, jnp.arange(n_blocks) * bm, side="right"), MOE_EXPERTS - 1)
    slot = slot.astype(jnp.int32)
    slot_tok = jnp.zeros((n_blocks * bm,), jnp.int32).at[slot].set(
        jnp.arange(n_tok * MOE_TOP_K, dtype=jnp.int32) // MOE_TOP_K, unique_indices=True)
    return (slot, slot_tok.reshape(n_blocks, bm), blk_e.astype(jnp.int32),
            (pad_end[-1:] // bm).astype(jnp.int32))


def kernel(x, c, ctx, c_ctx, w_ada, b_ada, norm_w, mlp_w_in, mlp_g_v, mlp_w_s, mlp_b_s, mlp_w_out,
           ffn_w1, ffn_w3, ffn_w2, na_w_qkv, na_g_q, na_g_k, na_rpb, na_w_o,
           moe_w_router, moe_w1, moe_w3, moe_w2):
    nb, s, d = x.shape
    n_ctx = ctx.shape[1]
    assert nb + 1 <= MOD_ROWS and w_ada.shape[0] == 2
    inner = mlp_w_out.shape[1]
    assert inner // MLP_GROUPS == LANE and d % NA_HEAD_DIM == 0

    cc = jnp.zeros((MOD_ROWS, d), F32).at[:nb].set(c).at[nb].set(c_ctx)
    mod = _ada_mod(cc, w_ada, b_ada)
    ctx_flat = ctx.reshape(1, nb * n_ctx, d)

    nw0 = norm_w[0]
    w_in = mlp_w_in[0].astype(BF16)
    w_s = mlp_w_s[0].astype(BF16)
    w_out = mlp_w_out[0].astype(BF16)
    g_v = mlp_g_v[0][None]
    b_full = jnp.repeat(mlp_b_s[0].T, inner // MLP_GROUPS, axis=1)
    mix = functools.partial(_mixer, mod=mod[0], nw=nw0[0][None], w_in=w_in, g_v=g_v, w_s=w_s, b_full=b_full,
                            w_out=w_out, tm=256)
    n_exp, _, moe_dim = moe_w1.shape[1:]
    x1, e_w2 = mix(x, mod_row=None, side=(moe_w2[0].reshape(n_exp * moe_dim, d),))
    xc1, = mix(ctx_flat, mod_row=nb)
    ffn = functools.partial(_ffn, mod=mod[0], nw=nw0[1][None], w1=ffn_w1[0].astype(BF16),
                            w3=ffn_w3[0].astype(BF16), w2=ffn_w2[0].astype(BF16), tm=512, tf=512)
    x1, e_w1, e_w3 = ffn(x1, mod_row=None, side=(moe_w1[0].reshape(n_exp * d, moe_dim),
                                                  moe_w3[0].reshape(n_exp * d, moe_dim)))
    xc1, = ffn(xc1, mod_row=nb)

    nw1 = norm_w[1]
    w_qkv = na_w_qkv[0].astype(BF16)
    qkv_fn = functools.partial(_qkv, mod=mod[1], nw=nw1[0][None], w_qkv=w_qkv, g_q=na_g_q[0][None],
                               g_k=na_g_k[0][None], tm=512)
    qkv = qkv_fn(x1, mod_row=None, j0=0)
    kvc = qkv_fn(xc1, mod_row=nb, j0=1)
    o = _attention(qkv, kvc, _bias_table(na_rpb[0]))

    wr = jnp.zeros((d, LANE), F32).at[:, :MOE_EXPERTS].set(moe_w_router[0])
    wr_hi, wr_lo = _split_bf16(wr)
    x2, toks, route = _proj_route(o, x1, mod[1], nw1[1][None], na_w_o[0].astype(BF16), wr_hi, wr_lo, tm=512)

    bm = 512
    n_tok = nb * s
    route_flat = route.reshape(n_tok, LANE)
    slot, slot_tok, blk_e, n_used = _moe_plan(route_flat, bm)
    ys = _experts(blk_e, n_used, slot_tok, toks.reshape(n_tok, d), e_w1.reshape(n_exp, d, moe_dim),
                  e_w3.reshape(n_exp, d, moe_dim), e_w2.reshape(n_exp, moe_dim, d), bm=bm, tf=min(1024, moe_dim))
    return _combine(slot, route, x2, mod[1], ys, rows=256)
```

```python
import functools

import numpy as np
import jax
import jax.numpy as jnp
from jax import lax
from jax.experimental import pallas as pl
from jax.experimental.pallas import tpu as pltpu

GRID_W = 64
N_MOD = 6
MLP_CHUNK = 128
MLP_GROUPS = 16
NA_HEAD_DIM = 128
NA_ROWS = 8
NA_COLS = 16
MOE_EXPERTS = 8
MOE_TOP_K = 2
EPS = 1e-6
NEG_INF = -1e30

LANE = 128
SUB_ROWS = 128
ROW_GROUP = 8
MOD_ROWS = 16
VMEM_LIMIT = 56 * 1024 * 1024

F32 = jnp.float32
BF16 = jnp.bfloat16


def _dot(a, b):
    return jnp.dot(a, b, preferred_element_type=F32)


def _dot_t(a, b):
    return lax.dot_general(a, b, (((1,), (1,)), ((), ())), preferred_element_type=F32)


def _split_bf16(a):
    hi = a.astype(BF16)
    lo = (a - hi.astype(F32)).astype(BF16)
    return hi, lo


def _silu(a):
    return a * jax.nn.sigmoid(a)


def _norm_mod(x, nw, shift, scale):
    ms = jnp.mean(x * x, axis=-1, keepdims=True)
    y = x * lax.rsqrt(ms + EPS)
    return (y * nw) * (1.0 + scale) + shift


def _mod_slices(mod_ref, row, d, first):
    base = 0 if first else 3
    return tuple(mod_ref[pl.ds(row, 1), (base + k) * d:(base + k + 1) * d] for k in range(3))


def _skewed(stages, n):
    state = [None] * n
    for t in range(n + len(stages) - 1):
        for s in reversed(range(len(stages))):
            k = t - s
            if 0 <= k < n:
                state[k] = stages[s](k, state[k])


def _params(sem):
    return pltpu.CompilerParams(dimension_semantics=sem, vmem_limit_bytes=VMEM_LIMIT)


def _side_rows(rows, n_steps):
    return next(r for r in range(16, rows + 1, 16) if rows % r == 0 and rows // r <= n_steps)


def _side_specs(side, n_steps, step_of):
    in_specs, out_specs, out_shapes = [], [], []
    for w in side:
        rows, cols = w.shape
        r = _side_rows(rows, n_steps)
        index = lambda *ids, last=rows // r - 1: (jnp.minimum(step_of(*ids), last), 0)
        in_specs.append(pl.BlockSpec((r, cols), index))
        out_specs.append(pl.BlockSpec((r, cols), index))
        out_shapes.append(jax.ShapeDtypeStruct(w.shape, BF16))
    return in_specs, out_specs, out_shapes


def _side_cast(side_in, side_out):
    for src, dst in zip(side_in, side_out):
        dst[...] = src[...].astype(BF16)


def _ada_kernel(a_ref, w_ref, b_ref, o_ref):
    a_hi, a_lo = _split_bf16(_silu(a_ref[...]))
    w_hi, w_lo = _split_bf16(w_ref[...])
    o_ref[...] = _dot(a_hi, w_hi) + _dot(a_lo, w_hi) + _dot(a_hi, w_lo) + b_ref[...]


def _ada_mod(cc, w_ada, b_ada, tn=1024):
    nl, d, n = w_ada.shape
    tn = min(tn, d)
    return pl.pallas_call(
        _ada_kernel,
        grid=(nl, n // tn),
        in_specs=[
            pl.BlockSpec((MOD_ROWS, d), lambda l, j: (0, 0)),
            pl.BlockSpec((None, d, tn), lambda l, j: (l, 0, j)),
            pl.BlockSpec((None, 1, tn), lambda l, j: (l, 0, j)),
        ],
        out_specs=pl.BlockSpec((None, MOD_ROWS, tn), lambda l, j: (l, 0, j)),
        out_shape=jax.ShapeDtypeStruct((nl, MOD_ROWS, n), F32),
        compiler_params=_params(("arbitrary", "arbitrary")),
        name="ada_mod",
    )(cc, w_ada, b_ada.reshape(nl, 1, n))


def _mixer_kernel(x_ref, mod_ref, nw_ref, win_ref, gv_ref, ws_ref, bs_ref, wout_ref, *rest,
                  mod_row, col_chunk, n_side):
    side_in, (o_ref, *side_out), (u_s, v_s, m_s) = rest[:n_side], rest[n_side:2 * n_side + 1], rest[2 * n_side + 1:]
    _side_cast(side_in, side_out)
    tm, d = x_ref.shape
    inner = u_s.shape[1]
    gd = inner // MLP_GROUPS
    row = pl.program_id(0) if mod_row is None else mod_row
    sh, sc, gate = _mod_slices(mod_ref, row, d, True)
    x = x_ref[...]
    h = _norm_mod(x, nw_ref[...], sh, sc).astype(BF16)
    ss = jnp.zeros((tm, 1), F32)
    for c in range(2 * inner // col_chunk):
        lo = c * col_chunk
        z = _dot(h, win_ref[:, lo:lo + col_chunk])
        z = 0.5 * z * (1.0 + lax.erf(z * (2.0 ** -0.5)))
        if lo < inner:
            u_s[:, lo:lo + col_chunk] = z
        else:
            v_s[:, lo - inner:lo - inner + col_chunk] = z
            ss = ss + jnp.sum(z * z, axis=-1, keepdims=True)
    rinv = lax.rsqrt(ss / inner + EPS)
    for n in range(tm // MLP_CHUNK):
        rows = slice(n * MLP_CHUNK, (n + 1) * MLP_CHUNK)
        for g in range(MLP_GROUPS):
            cols = slice(g * gd, (g + 1) * gd)
            vb = ((v_s[rows, cols] * rinv[rows]) * gv_ref[:, cols]).astype(BF16)
            mixed = _dot(ws_ref[g], vb) + bs_ref[:, cols]
            m_s[rows, cols] = (u_s[rows, cols] * mixed).astype(BF16)
    y = _dot(m_s[...], wout_ref[...])
    o_ref[...] = x + gate * y


def _mixer(x, mod, nw, w_in, g_v, w_s, b_full, w_out, *, mod_row, tm, side=()):
    nb, s, d = x.shape
    inner = w_out.shape[0]
    tm = min(tm, s)
    ni = s // tm
    const = lambda *shape: pl.BlockSpec(shape, lambda b, i: (0,) * len(shape))
    side_in, side_out, side_shapes = _side_specs(side, nb * ni, lambda b, i: b * ni + i)
    return pl.pallas_call(
        functools.partial(_mixer_kernel, mod_row=mod_row, col_chunk=min(512, inner), n_side=len(side)),
        grid=(nb, ni),
        in_specs=[
            pl.BlockSpec((None, tm, d), lambda b, i: (b, i, 0)),
            const(*mod.shape), const(1, d), const(d, 2 * inner), const(1, inner),
            const(*w_s.shape), const(MLP_CHUNK, inner), const(inner, d), *side_in,
        ],
        out_specs=[pl.BlockSpec((None, tm, d), lambda b, i: (b, i, 0)), *side_out],
        out_shape=[jax.ShapeDtypeStruct(x.shape, F32), *side_shapes],
        scratch_shapes=[pltpu.VMEM((tm, inner), F32), pltpu.VMEM((tm, inner), F32),
                        pltpu.VMEM((tm, inner), BF16)],
        compiler_params=_params(("arbitrary", "arbitrary")),
        name="gmlp_mixer",
    )(x, mod, nw, w_in, g_v, w_s, b_full, w_out, *side)


def _ffn_kernel(x_ref, mod_ref, nw_ref, w1_ref, w3_ref, w2_ref, *rest, mod_row, n_side):
    side_in, (o_ref, *side_out), (t_s, acc_s) = rest[:n_side], rest[n_side:2 * n_side + 1], rest[2 * n_side + 1:]
    _side_cast(side_in, side_out)
    d = x_ref.shape[-1]
    f = pl.program_id(2)
    row = pl.program_id(0) if mod_row is None else mod_row
    sh, sc, gate = _mod_slices(mod_ref, row, d, False)

    def up(t):
        return (_silu(_dot(t, w1_ref[...])) * _dot(t, w3_ref[...])).astype(BF16)

    @pl.when(f == 0)
    def _():
        sub = min(SUB_ROWS, x_ref.shape[0])

        def norm(k, _):
            rows = pl.ds(k * sub, sub)
            t = _norm_mod(x_ref[rows, :], nw_ref[...], sh, sc).astype(BF16)
            t_s[rows, :] = t
            return t

        def down(k, hmid):
            acc_s[pl.ds(k * sub, sub), :] = _dot(hmid, w2_ref[...])

        _skewed((norm, lambda k, t: up(t), down), x_ref.shape[0] // sub)

    @pl.when(f > 0)
    def _():
        acc_s[...] += _dot(up(t_s[...]), w2_ref[...])

    @pl.when(f == pl.num_programs(2) - 1)
    def _():
        o_ref[...] = x_ref[...] + gate * acc_s[...]


def _ffn(x, mod, nw, w1, w3, w2, *, mod_row, tm, tf, side=()):
    nb, s, d = x.shape
    tm = min(tm, s)
    ni = s // tm
    nf = w1.shape[1] // tf
    side_in, side_out, side_shapes = _side_specs(side, nb * ni * nf, lambda b, i, f: (b * ni + i) * nf + f)
    return pl.pallas_call(
        functools.partial(_ffn_kernel, mod_row=mod_row, n_side=len(side)),
        grid=(nb, ni, nf),
        in_specs=[
            pl.BlockSpec((None, tm, d), lambda b, i, f: (b, i, 0)),
            pl.BlockSpec(mod.shape, lambda b, i, f: (0, 0)),
            pl.BlockSpec((1, d), lambda b, i, f: (0, 0)),
            pl.BlockSpec((d, tf), lambda b, i, f: (0, f)),
            pl.BlockSpec((d, tf), lambda b, i, f: (0, f)),
            pl.BlockSpec((tf, d), lambda b, i, f: (f, 0)),
            *side_in,
        ],
        out_specs=[pl.BlockSpec((None, tm, d), lambda b, i, f: (b, i, 0)), *side_out],
        out_shape=[jax.ShapeDtypeStruct(x.shape, F32), *side_shapes],
        scratch_shapes=[pltpu.VMEM((tm, d), BF16), pltpu.VMEM((tm, d), F32)],
        compiler_params=_params(("arbitrary", "arbitrary", "arbitrary")),
        name="ffn_swiglu",
    )(x, mod, nw, w1, w3, w2, *side)


def _qkv_kernel(x_ref, mod_ref, nw_ref, *rest, mod_row, j0):
    nj = 3 - j0
    w_refs, (gq_ref, gk_ref, o_ref) = rest[:nj], rest[nj:]
    tm, d = x_ref.shape
    row = pl.program_id(0) if mod_row is None else mod_row
    sh, sc, _ = _mod_slices(mod_ref, row, d, True)
    sub = min(SUB_ROWS, tm)

    def norm(k, _):
        return (_norm_mod(x_ref[pl.ds(k * sub, sub), :], nw_ref[...], sh, sc).astype(BF16),)

    def project(jj):
        def stage(k, state):
            return state[0], _dot(state[0], w_refs[jj][...])
        return stage

    def finish(jj):
        part = jj + j0

        def stage(k, state):
            h, y = state
            rows = pl.ds(k * sub, sub)
            if part == 2:
                o_ref[jj, rows, :] = y.astype(BF16)
                return (h,)
            g = gq_ref[...] if part == 0 else gk_ref[...]
            for hh in range(d // NA_HEAD_DIM):
                cols = slice(hh * NA_HEAD_DIM, (hh + 1) * NA_HEAD_DIM)
                blk = y[:, cols]
                ms = jnp.mean(blk * blk, axis=-1, keepdims=True)
                out = (blk * lax.rsqrt(ms + EPS)) * g
                if part == 0:
                    out = out * (NA_HEAD_DIM ** -0.5)
                o_ref[jj, rows, cols] = out.astype(BF16)
            return (h,)
        return stage

    stages = [norm]
    for jj in range(nj):
        stages += [project(jj), finish(jj)]
    _skewed(stages, tm // sub)


def _qkv(x, mod, nw, w_qkv, g_q, g_k, *, mod_row, tm, j0):
    nb, s, d = x.shape
    tm = min(tm, s)
    nj = 3 - j0
    const = lambda *shape: pl.BlockSpec(shape, lambda b, i: (0,) * len(shape))
    return pl.pallas_call(
        functools.partial(_qkv_kernel, mod_row=mod_row, j0=j0),
        grid=(nb, s // tm),
        in_specs=[
            pl.BlockSpec((None, tm, d), lambda b, i: (b, i, 0)),
            const(*mod.shape), const(1, d),
            *[pl.BlockSpec((d, d), lambda b, i, part=jj + j0: (0, part)) for jj in range(nj)],
            const(1, NA_HEAD_DIM), const(1, NA_HEAD_DIM),
        ],
        out_specs=pl.BlockSpec((nj, None, tm, d), lambda b, i: (0, b, i, 0)),
        out_shape=jax.ShapeDtypeStruct((nj, nb, s, d), BF16),
        compiler_params=_params(("arbitrary", "arbitrary")),
        name="qkv_proj",
    )(x, mod, nw, *([w_qkv] * nj), g_q, g_k)


def _bias_table(rpb):
    qcol = np.arange(GRID_W)[:, None]
    kcol = np.arange(GRID_W)[None, :]
    cs = np.clip(qcol - NA_COLS // 2, 0, GRID_W - NA_COLS)
    inwin = (kcol >= cs) & (kcol < cs + NA_COLS)
    dc = np.clip(kcol - qcol + NA_COLS - 1, 0, 2 * NA_COLS - 2)
    tab = jnp.where(inwin[None, None], rpb.astype(F32)[:, :, dc], NEG_INF)
    return jnp.concatenate([tab[:, :-1], tab[:, 1:]], axis=-1)


def _attn_kernel(q_ref, k_ref, v_ref, kc_ref, vc_ref, tab_ref, o_ref, s_s, p_s, *, kh):
    d = q_ref.shape[-1]
    heads = d // NA_HEAD_DIM
    rows = k_ref.shape[0] // GRID_W
    n_loc = kh * GRID_W
    rows_per_step = q_ref.shape[0] // GRID_W
    units = []
    for rr in range(rows_per_step):
        r = pl.program_id(1) * rows_per_step + rr
        rs = jnp.clip(r - kh // 2, 0, rows - kh)
        dr0 = rs - r + (NA_ROWS - 1)
        win = pl.ds(pl.multiple_of(rs * GRID_W, GRID_W), n_loc)
        qrows = slice(rr * GRID_W, (rr + 1) * GRID_W)
        for hh in range(heads):
            units.append((rr * heads + hh, hh, qrows, slice(hh * NA_HEAD_DIM, (hh + 1) * NA_HEAD_DIM), dr0, win))
    for u, hh, qrows, cols, dr0, win in units:
        qh = q_ref[qrows, cols]
        bias = jnp.concatenate([tab_ref[hh, pl.ds(dr0 + 2 * j, 1)][0] for j in range(kh // 2)], axis=-1)
        s_s[u, :, :n_loc] = _dot_t(qh, k_ref[win, cols]) + bias
        s_s[u, :, n_loc:] = _dot_t(qh, kc_ref[:, cols])
    dens = []
    for u in range(len(units)):
        s = s_s[u]
        p = jnp.exp(s - jnp.max(s, axis=-1, keepdims=True))
        dens.append(jnp.sum(p, axis=-1, keepdims=True))
        p_s[u] = p.astype(BF16)
    for u, hh, qrows, cols, dr0, win in units:
        o = _dot(p_s[u, :, :n_loc], v_ref[win, cols]) + _dot(p_s[u, :, n_loc:], vc_ref[:, cols])
        o_ref[qrows, cols] = (o / dens[u]).astype(BF16)


def _attention(qkv, kvc, tab):
    _, nb, s, d = qkv.shape
    n_ctx = kvc.shape[2] // nb
    rows = s // GRID_W
    kh = min(NA_ROWS, rows)
    once = pl.Buffered(1)
    rps = 2 if rows % 2 == 0 else 1
    n_units = rps * (d // NA_HEAD_DIM)
    return pl.pallas_call(
        functools.partial(_attn_kernel, kh=kh),
        grid=(nb, rows // rps),
        in_specs=[
            pl.BlockSpec((None, None, rps * GRID_W, d), lambda b, r: (0, b, r, 0)),
            pl.BlockSpec((None, None, s, d), lambda b, r: (1, b, 0, 0), pipeline_mode=once),
            pl.BlockSpec((None, None, s, d), lambda b, r: (2, b, 0, 0), pipeline_mode=once),
            pl.BlockSpec((None, None, n_ctx, d), lambda b, r: (0, 0, b, 0)),
            pl.BlockSpec((None, None, n_ctx, d), lambda b, r: (1, 0, b, 0)),
            pl.BlockSpec(tab.shape, lambda b, r: (0, 0, 0, 0), pipeline_mode=once),
        ],
        out_specs=pl.BlockSpec((None, rps * GRID_W, d), lambda b, r: (b, r, 0)),
        out_shape=jax.ShapeDtypeStruct((nb, s, d), BF16),
        scratch_shapes=[pltpu.VMEM((n_units, GRID_W, kh * GRID_W + n_ctx), F32),
                        pltpu.VMEM((n_units, GRID_W, kh * GRID_W + n_ctx), BF16)],
        compiler_params=_params(("arbitrary", "arbitrary")),
        name="nbr_attention",
    )(qkv, qkv, qkv, kvc, kvc, tab)


def _proj_route_kernel(o_ref, x_ref, mod_ref, nw_ref, wo_ref, wrh_ref, wrl_ref, x2_ref, tok_ref, route_ref):
    tm, d = x_ref.shape
    row = pl.program_id(0)
    _, _, gate1 = _mod_slices(mod_ref, row, d, True)
    sh, sc, _ = _mod_slices(mod_ref, row, d, False)
    sub = min(SUB_ROWS, tm)
    lane = lax.broadcasted_iota(jnp.int32, (sub, LANE), 1)

    def project(k, _):
        rows = pl.ds(k * sub, sub)
        return _dot(o_ref[rows, :], wo_ref[...])

    def residual_norm(k, y):
        rows = pl.ds(k * sub, sub)
        x2 = x_ref[rows, :] + gate1 * y
        x2_ref[rows, :] = x2
        t = _norm_mod(x2, nw_ref[...], sh, sc)
        tok_ref[rows, :] = t
        return _split_bf16(t)

    def router(k, t_split):
        t_hi, t_lo = t_split
        return _dot(t_hi, wrh_ref[...]) + _dot(t_lo, wrh_ref[...]) + _dot(t_hi, wrl_ref[...])

    def top2(k, logits):
        logits = jnp.where(lane < MOE_EXPERTS, logits, -jnp.inf)
        m1 = jnp.max(logits, axis=-1, keepdims=True)
        i1 = jnp.min(jnp.where(logits == m1, lane, LANE), axis=-1, keepdims=True)
        rest = jnp.where(lane == i1, -jnp.inf, logits)
        m2 = jnp.max(rest, axis=-1, keepdims=True)
        i2 = jnp.min(jnp.where(rest == m2, lane, LANE), axis=-1, keepdims=True)
        e = jnp.exp(m2 - m1)
        den = 1.0 + e
        route_ref[pl.ds(k * sub, sub), :] = jnp.where(
            lane == 0, i1.astype(F32),
            jnp.where(lane == 1, i2.astype(F32),
                      jnp.where(lane == 2, 1.0 / den, jnp.where(lane == 3, e / den, 0.0))))

    _skewed((project, residual_norm, router, top2), tm // sub)


def _proj_route(o, x, mod, nw, w_o, wr_hi, wr_lo, *, tm):
    nb, s, d = x.shape
    const = lambda *shape: pl.BlockSpec(shape, lambda b, i: (0,) * len(shape))
    tile = lambda w: pl.BlockSpec((None, tm, w), lambda b, i: (b, i, 0))
    return pl.pallas_call(
        _proj_route_kernel,
        grid=(nb, s // tm),
        in_specs=[tile(d), tile(d), const(*mod.shape), const(1, d), const(d, d), const(d, LANE), const(d, LANE)],
        out_specs=[tile(d), tile(d), tile(LANE)],
        out_shape=[jax.ShapeDtypeStruct(x.shape, F32), jax.ShapeDtypeStruct(x.shape, F32),
                   jax.ShapeDtypeStruct((nb, s, LANE), F32)],
        compiler_params=_params(("arbitrary", "arbitrary")),
        name="attn_proj_route",
    )(o, x, mod, nw, w_o, wr_hi, wr_lo)


def _expert_kernel(be_ref, nu_ref, st_ref, stn_ref, tok_hbm, w1_ref, w3_ref, w2_ref, y_ref,
                   xf_s, xb_s, acc_s, sems, *, groups_per_step):
    del be_ref
    b = pl.program_id(0)
    f = pl.program_id(1)
    bm = xb_s.shape[0]
    n_groups = xf_s.shape[1]
    last = f == pl.num_programs(1) - 1
    n_used = nu_ref[0]
    used = b < n_used
    cur = b % 2

    def row_copy(tok, grp, u, buf):
        return pltpu.make_async_copy(tok_hbm.at[pl.ds(tok, 1)], xf_s.at[buf, grp, pl.ds(u, 1)], sems.at[buf])

    def request_group(idx_ref, grp, buf):
        for u in range(ROW_GROUP):
            row_copy(idx_ref[0, grp * ROW_GROUP + u], grp, u, buf).start(priority=u % 2)

    def wait_all(buf):
        def wait(i, carry):
            row_copy(0, 0, 0, buf).wait()
            return carry
        lax.fori_loop(0, n_groups * ROW_GROUP, wait, 0, unroll=8)

    def request(idx_ref, buf, g0, n):
        def body(i, carry):
            request_group(idx_ref, g0 + i, buf)
            return carry
        lax.fori_loop(0, n, body, 0)

    def up(xb):
        return (_silu(_dot(xb, w1_ref[...])) * _dot(xb, w3_ref[...])).astype(BF16)

    @pl.when(used)
    def _():
        @pl.when(jnp.logical_and(b == 0, f == 0))
        def _():
            request(st_ref, 0, 0, n_groups)

        @pl.when(jnp.logical_and(f < n_groups // groups_per_step, b + 1 < n_used))
        def _():
            request(stn_ref, 1 - cur, f * groups_per_step, groups_per_step)

        @pl.when(f == 0)
        def _():
            wait_all(cur)
            sub = min(SUB_ROWS, bm)
            sub_groups = sub // ROW_GROUP

            def cast(k, _):
                xb = xf_s[cur, pl.ds(k * sub_groups, sub_groups)].reshape(sub, xb_s.shape[1]).astype(BF16)
                xb_s[pl.ds(k * sub, sub), :] = xb
                return xb

            def down(k, hmid):
                acc_s[pl.ds(k * sub, sub), :] = _dot(hmid, w2_ref[...])

            _skewed((cast, lambda k, xb: up(xb), down), bm // sub)

        @pl.when(f > 0)
        def _():
            acc_s[...] += _dot(up(xb_s[...]), w2_ref[...])

        @pl.when(last)
        def _():
            y_ref[...] = acc_s[...]

    @pl.when(jnp.logical_and(jnp.logical_not(used), last))
    def _():
        y_ref[...] = jnp.zeros_like(y_ref)


def _experts(blk_e, n_used, slot_tok, toks, w1, w3, w2, *, bm, tf):
    n_blocks = slot_tok.shape[0]
    d = toks.shape[1]
    nf = w1.shape[2] // tf
    n_groups = bm // ROW_GROUP
    issue_steps = next(n for n in (8, 4, 2, 1) if n <= nf)
    groups_per_step = n_groups // issue_steps
    slot_tok = slot_tok.reshape(n_blocks, 1, bm)

    def wcol(b, f, be, nu):
        return (be[b], 0, jnp.where(b < nu[0], f, nf - 1))

    def wrow(b, f, be, nu):
        return (be[b], jnp.where(b < nu[0], f, nf - 1), 0)

    return pl.pallas_call(
        functools.partial(_expert_kernel, groups_per_step=groups_per_step),
        grid_spec=pltpu.PrefetchScalarGridSpec(
            num_scalar_prefetch=2,
            grid=(n_blocks, nf),
            in_specs=[
                pl.BlockSpec((None, 1, n_groups * ROW_GROUP), lambda b, f, be, nu: (b, 0, 0),
                             memory_space=pltpu.SMEM),
                pl.BlockSpec((None, 1, n_groups * ROW_GROUP),
                             lambda b, f, be, nu: (jnp.minimum(b + 1, n_blocks - 1), 0, 0),
                             memory_space=pltpu.SMEM),
                pl.BlockSpec(memory_space=pl.ANY),
                pl.BlockSpec((None, d, tf), wcol),
                pl.BlockSpec((None, d, tf), wcol),
                pl.BlockSpec((None, tf, d), wrow),
            ],
            out_specs=pl.BlockSpec((bm, d), lambda b, f, be, nu: (b, 0)),
            scratch_shapes=[pltpu.VMEM((2, n_groups, ROW_GROUP, d), F32), pltpu.VMEM((bm, d), BF16),
                            pltpu.VMEM((bm, d), F32),
                            pltpu.SemaphoreType.DMA((2,))],
        ),
        out_shape=jax.ShapeDtypeStruct((n_blocks * bm, d), F32),
        compiler_params=_params(("arbitrary", "arbitrary")),
        name="moe_experts",
    )(blk_e, n_used, slot_tok, slot_tok, toks, w1, w3, w2)


def _combine_kernel(slot_ref, slotn_ref, route_ref, x_ref, mod_ref, y_hbm, o_ref, y_s, sems):
    rows, d = x_ref.shape
    step = pl.program_id(0) * pl.num_programs(1) + pl.program_id(1)
    n_steps = pl.num_programs(0) * pl.num_programs(1)
    cur = step % 2

    def row_copy(s, buf, k, grp, u):
        return pltpu.make_async_copy(y_hbm.at[pl.ds(s, 1)], y_s.at[buf, k, grp, pl.ds(u, 1)], sems.at[buf])

    def request(idx_ref, buf):
        def body(grp, carry):
            for u in range(ROW_GROUP):
                for k in range(MOE_TOP_K):
                    row_copy(idx_ref[0, 0, MOE_TOP_K * (grp * ROW_GROUP + u) + k], buf, k, grp, u).start(priority=k)
            return carry
        lax.fori_loop(0, rows // ROW_GROUP, body, 0)

    @pl.when(step == 0)
    def _():
        request(slot_ref, 0)

    @pl.when(step + 1 < n_steps)
    def _():
        request(slotn_ref, 1 - cur)

    def wait(i, carry):
        row_copy(0, cur, 0, 0, 0).wait()
        return carry

    lax.fori_loop(0, rows * MOE_TOP_K, wait, 0, unroll=8)
    _, _, gate2 = _mod_slices(mod_ref, pl.program_id(0), d, False)
    route = route_ref[...]
    f = route[:, 2:3] * y_s[cur, 0].reshape(rows, d) + route[:, 3:4] * y_s[cur, 1].reshape(rows, d)
    o_ref[...] = x_ref[...] + gate2 * f


def _combine(slot, route, x, mod, y, *, rows):
    nb, s, d = x.shape
    steps = s // rows
    idx_spec = lambda ahead: pl.BlockSpec(
        (1, 1, MOE_TOP_K * rows), lambda b, i: (jnp.minimum(b * steps + i + ahead, nb * steps - 1), 0, 0),
        memory_space=pltpu.SMEM)
    slot = slot.reshape(nb * steps, 1, MOE_TOP_K * rows)
    return pl.pallas_call(
        _combine_kernel,
        grid=(nb, steps),
        in_specs=[
            idx_spec(0), idx_spec(1),
            pl.BlockSpec((None, rows, LANE), lambda b, i: (b, i, 0)),
            pl.BlockSpec((None, rows, d), lambda b, i: (b, i, 0)),
            pl.BlockSpec(mod.shape, lambda b, i: (0, 0)),
            pl.BlockSpec(memory_space=pl.ANY),
        ],
        out_specs=pl.BlockSpec((None, rows, d), lambda b, i: (b, i, 0)),
        out_shape=jax.ShapeDtypeStruct(x.shape, F32),
        scratch_shapes=[pltpu.VMEM((2, MOE_TOP_K, rows // ROW_GROUP, ROW_GROUP, d), F32),
                        pltpu.SemaphoreType.DMA((2,))],
        compiler_params=_params(("arbitrary", "arbitrary")),
        name="moe_combine",
    )(slot, slot, route, x, mod, y)


def _moe_plan(route, bm):
    n_tok = route.shape[0]
    e_flat = route[:, :MOE_TOP_K].astype(jnp.int32).reshape(-1)
    onehot = (e_flat[:, None] == jnp.arange(MOE_EXPERTS)[None, :]).astype(jnp.int32)
    csum = jnp.cumsum(onehot, axis=0)
    counts = csum[-1]
    rank = jnp.sum((csum - onehot) * onehot, axis=1)
    padded = (counts + bm - 1) // bm * bm
    pad_end = jnp.cumsum(padded)
    slot = (pad_end - padded)[e_flat] + rank
    n_blocks = (n_tok * MOE_TOP_K) // bm + MOE_EXPERTS
    blk_e = jnp.minimum(jnp.searchsorted(pad_end, jnp.arange(n_blocks) * bm, side="right"), MOE_EXPERTS - 1)
    slot = slot.astype(jnp.int32)
    slot_tok = jnp.zeros((n_blocks * bm,), jnp.int32).at[slot].set(
        jnp.arange(n_tok * MOE_TOP_K, dtype=jnp.int32) // MOE_TOP_K, unique_indices=True)
    return (slot, slot_tok.reshape(n_blocks, bm), blk_e.astype(jnp.int32),
            (pad_end[-1:] // bm).astype(jnp.int32))


def kernel(x, c, ctx, c_ctx, w_ada, b_ada, norm_w, mlp_w_in, mlp_g_v, mlp_w_s, mlp_b_s, mlp_w_out,
           ffn_w1, ffn_w3, ffn_w2, na_w_qkv, na_g_q, na_g_k, na_rpb, na_w_o,
           moe_w_router, moe_w1, moe_w3, moe_w2):
    nb, s, d = x.shape
    n_ctx = ctx.shape[1]
    assert nb + 1 <= MOD_ROWS and w_ada.shape[0] == 2
    inner = mlp_w_out.shape[1]
    assert inner // MLP_GROUPS == LANE and d % NA_HEAD_DIM == 0

    cc = jnp.zeros((MOD_ROWS, d), F32).at[:nb].set(c).at[nb].set(c_ctx)
    mod = _ada_mod(cc, w_ada, b_ada)
    ctx_flat = ctx.reshape(1, nb * n_ctx, d)

    nw0 = norm_w[0]
    w_in = mlp_w_in[0].astype(BF16)
    w_s = mlp_w_s[0].astype(BF16)
    w_out = mlp_w_out[0].astype(BF16)
    g_v = mlp_g_v[0][None]
    b_full = jnp.repeat(mlp_b_s[0].T, inner // MLP_GROUPS, axis=1)
    mix = functools.partial(_mixer, mod=mod[0], nw=nw0[0][None], w_in=w_in, g_v=g_v, w_s=w_s, b_full=b_full,
                            w_out=w_out, tm=256)
    n_exp, _, moe_dim = moe_w1.shape[1:]
    x1, e_w2 = mix(x, mod_row=None, side=(moe_w2[0].reshape(n_exp * moe_dim, d),))
    xc1, = mix(ctx_flat, mod_row=nb)
    ffn = functools.partial(_ffn, mod=mod[0], nw=nw0[1][None], w1=ffn_w1[0].astype(BF16),
                            w3=ffn_w3[0].astype(BF16), w2=ffn_w2[0].astype(BF16), tm=512, tf=512)
    x1, e_w1, e_w3 = ffn(x1, mod_row=None, side=(moe_w1[0].reshape(n_exp * d, moe_dim),
                                                  moe_w3[0].reshape(n_exp * d, moe_dim)))
    xc1, = ffn(xc1, mod_row=nb)

    nw1 = norm_w[1]
    w_qkv = na_w_qkv[0].astype(BF16)
    qkv_fn = functools.partial(_qkv, mod=mod[1], nw=nw1[0][None], w_qkv=w_qkv, g_q=na_g_q[0][None],
                               g_k=na_g_k[0][None], tm=512)
    qkv = qkv_fn(x1, mod_row=None, j0=0)
    kvc = qkv_fn(xc1, mod_row=nb, j0=1)
    o = _attention(qkv, kvc, _bias_table(na_rpb[0]))

    wr = jnp.zeros((d, LANE), F32).at[:, :MOE_EXPERTS].set(moe_w_router[0])
    wr_hi, wr_lo = _split_bf16(wr)
    x2, toks, route = _proj_route(o, x1, mod[1], nw1[1][None], na_w_o[0].astype(BF16), wr_hi, wr_lo, tm=512)

    bm = 512
    n_tok = nb * s
    route_flat = route.reshape(n_tok, LANE)
    slot, slot_tok, blk_e, n_used = _moe_plan(route_flat, bm)
    ys = _experts(blk_e, n_used, slot_tok, toks.reshape(n_tok, d), e_w1.reshape(n_exp, d, moe_dim),
                  e_w3.reshape(n_exp, d, moe_dim), e_w2.reshape(n_exp, moe_dim, d), bm=bm, tf=min(1024, moe_dim))
    return _combine(slot, route, x2, mod[1], ys, rows=256)
```

```python
import functools

import numpy as np
import jax
import jax.numpy as jnp
from jax import lax
from jax.experimental import pallas as pl
from jax.experimental.pallas import tpu as pltpu

GRID_W = 64
N_MOD = 6
MLP_CHUNK = 128
MLP_GROUPS = 16
NA_HEAD_DIM = 128
NA_ROWS = 8
NA_COLS = 16
MOE_EXPERTS = 8
MOE_TOP_K = 2
EPS = 1e-6
NEG_INF = -1e30

LANE = 128
SUB_ROWS = 128
ROW_GROUP = 8
MOD_ROWS = 16
VMEM_LIMIT = 56 * 1024 * 1024

F32 = jnp.float32
BF16 = jnp.bfloat16


def _dot(a, b):
    return jnp.dot(a, b, preferred_element_type=F32)


def _dot_t(a, b):
    return lax.dot_general(a, b, (((1,), (1,)), ((), ())), preferred_element_type=F32)


def _split_bf16(a):
    hi = a.astype(BF16)
    lo = (a - hi.astype(F32)).astype(BF16)
    return hi, lo


def _silu(a):
    return a * jax.nn.sigmoid(a)


def _norm_mod(x, nw, shift, scale):
    ms = jnp.mean(x * x, axis=-1, keepdims=True)
    y = x * lax.rsqrt(ms + EPS)
    return (y * nw) * (1.0 + scale) + shift


def _mod_slices(mod_ref, row, d, first):
    base = 0 if first else 3
    return tuple(mod_ref[pl.ds(row, 1), (base + k) * d:(base + k + 1) * d] for k in range(3))


def _skewed(stages, n):
    state = [None] * n
    for t in range(n + len(stages) - 1):
        for s in reversed(range(len(stages))):
            k = t - s
            if 0 <= k < n:
                state[k] = stages[s](k, state[k])


def _params(sem):
    return pltpu.CompilerParams(dimension_semantics=sem, vmem_limit_bytes=VMEM_LIMIT)


def _side_rows(rows, n_steps):
    return next(r for r in range(16, rows + 1, 16) if rows % r == 0 and rows // r <= n_steps)


def _side_specs(side, n_steps, step_of):
    in_specs, out_specs, out_shapes = [], [], []
    for w in side:
        rows, cols = w.shape
        r = _side_rows(rows, n_steps)
        index = lambda *ids, last=rows // r - 1: (jnp.minimum(step_of(*ids), last), 0)
        in_specs.append(pl.BlockSpec((r, cols), index))
        out_specs.append(pl.BlockSpec((r, cols), index))
        out_shapes.append(jax.ShapeDtypeStruct(w.shape, BF16))
    return in_specs, out_specs, out_shapes


def _side_cast(side_in, side_out):
    for src, dst in zip(side_in, side_out):
        dst[...] = src[...].astype(BF16)


def _ada_kernel(a_ref, w_ref, b_ref, o_ref):
    a_hi, a_lo = _split_bf16(_silu(a_ref[...]))
    w_hi, w_lo = _split_bf16(w_ref[...])
    o_ref[...] = _dot(a_hi, w_hi) + _dot(a_lo, w_hi) + _dot(a_hi, w_lo) + b_ref[...]


def _ada_mod(cc, w_ada, b_ada, tn=1024):
    nl, d, n = w_ada.shape
    tn = min(tn, d)
    return pl.pallas_call(
        _ada_kernel,
        grid=(nl, n // tn),
        in_specs=[
            pl.BlockSpec((MOD_ROWS, d), lambda l, j: (0, 0)),
            pl.BlockSpec((None, d, tn), lambda l, j: (l, 0, j)),
            pl.BlockSpec((None, 1, tn), lambda l, j: (l, 0, j)),
        ],
        out_specs=pl.BlockSpec((None, MOD_ROWS, tn), lambda l, j: (l, 0, j)),
        out_shape=jax.ShapeDtypeStruct((nl, MOD_ROWS, n), F32),
        compiler_params=_params(("arbitrary", "arbitrary")),
        name="ada_mod",
    )(cc, w_ada, b_ada.reshape(nl, 1, n))


def _mixer_kernel(x_ref, mod_ref, nw_ref, win_ref, gv_ref, ws_ref, bs_ref, wout_ref, *rest,
                  mod_row, col_chunk, n_side):
    side_in, (o_ref, *side_out), (u_s, v_s, m_s) = rest[:n_side], rest[n_side:2 * n_side + 1], rest[2 * n_side + 1:]
    _side_cast(side_in, side_out)
    tm, d = x_ref.shape
    inner = u_s.shape[1]
    gd = inner // MLP_GROUPS
    row = pl.program_id(0) if mod_row is None else mod_row
    sh, sc, gate = _mod_slices(mod_ref, row, d, True)
    x = x_ref[...]
    h = _norm_mod(x, nw_ref[...], sh, sc).astype(BF16)
    ss = jnp.zeros((tm, 1), F32)
    for c in range(2 * inner // col_chunk):
        lo = c * col_chunk
        z = _dot(h, win_ref[:, lo:lo + col_chunk])
        z = 0.5 * z * (1.0 + lax.erf(z * (2.0 ** -0.5)))
        if lo < inner:
            u_s[:, lo:lo + col_chunk] = z
        else:
            v_s[:, lo - inner:lo - inner + col_chunk] = z
            ss = ss + jnp.sum(z * z, axis=-1, keepdims=True)
    rinv = lax.rsqrt(ss / inner + EPS)
    for n in range(tm // MLP_CHUNK):
        rows = slice(n * MLP_CHUNK, (n + 1) * MLP_CHUNK)
        for g in range(MLP_GROUPS):
            cols = slice(g * gd, (g + 1) * gd)
            vb = ((v_s[rows, cols] * rinv[rows]) * gv_ref[:, cols]).astype(BF16)
            mixed = _dot(ws_ref[g], vb) + bs_ref[:, cols]
            m_s[rows, cols] = (u_s[rows, cols] * mixed).astype(BF16)
    y = _dot(m_s[...], wout_ref[...])
    o_ref[...] = x + gate * y


def _mixer(x, mod, nw, w_in, g_v, w_s, b_full, w_out, *, mod_row, tm, side=()):
    nb, s, d = x.shape
    inner = w_out.shape[0]
    tm = min(tm, s)
    ni = s // tm
    const = lambda *shape: pl.BlockSpec(shape, lambda b, i: (0,) * len(shape))
    side_in, side_out, side_shapes = _side_specs(side, nb * ni, lambda b, i: b * ni + i)
    return pl.pallas_call(
        functools.partial(_mixer_kernel, mod_row=mod_row, col_chunk=min(512, inner), n_side=len(side)),
        grid=(nb, ni),
        in_specs=[
            pl.BlockSpec((None, tm, d), lambda b, i: (b, i, 0)),
            const(*mod.shape), const(1, d), const(d, 2 * inner), const(1, inner),
            const(*w_s.shape), const(MLP_CHUNK, inner), const(inner, d), *side_in,
        ],
        out_specs=[pl.BlockSpec((None, tm, d), lambda b, i: (b, i, 0)), *side_out],
        out_shape=[jax.ShapeDtypeStruct(x.shape, F32), *side_shapes],
        scratch_shapes=[pltpu.VMEM((tm, inner), F32), pltpu.VMEM((tm, inner), F32),
                        pltpu.VMEM((tm, inner), BF16)],
        compiler_params=_params(("arbitrary", "arbitrary")),
        name="gmlp_mixer",
    )(x, mod, nw, w_in, g_v, w_s, b_full, w_out, *side)


def _ffn_kernel(x_ref, mod_ref, nw_ref, w1_ref, w3_ref, w2_ref, *rest, mod_row, n_side):
    side_in, (o_ref, *side_out), (t_s, acc_s) = rest[:n_side], rest[n_side:2 * n_side + 1], rest[2 * n_side + 1:]
    _side_cast(side_in, side_out)
    d = x_ref.shape[-1]
    f = pl.program_id(2)
    row = pl.program_id(0) if mod_row is None else mod_row
    sh, sc, gate = _mod_slices(mod_ref, row, d, False)

    def up(t):
        return (_silu(_dot(t, w1_ref[...])) * _dot(t, w3_ref[...])).astype(BF16)

    @pl.when(f == 0)
    def _():
        sub = min(SUB_ROWS, x_ref.shape[0])

        def norm(k, _):
            rows = pl.ds(k * sub, sub)
            t = _norm_mod(x_ref[rows, :], nw_ref[...], sh, sc).astype(BF16)
            t_s[rows, :] = t
            return t

        def down(k, hmid):
            acc_s[pl.ds(k * sub, sub), :] = _dot(hmid, w2_ref[...])

        _skewed((norm, lambda k, t: up(t), down), x_ref.shape[0] // sub)

    @pl.when(f > 0)
    def _():
        acc_s[...] += _dot(up(t_s[...]), w2_ref[...])

    @pl.when(f == pl.num_programs(2) - 1)
    def _():
        o_ref[...] = x_ref[...] + gate * acc_s[...]


def _ffn(x, mod, nw, w1, w3, w2, *, mod_row, tm, tf, side=()):
    nb, s, d = x.shape
    tm = min(tm, s)
    ni = s // tm
    nf = w1.shape[1] // tf
    side_in, side_out, side_shapes = _side_specs(side, nb * ni * nf, lambda b, i, f: (b * ni + i) * nf + f)
    return pl.pallas_call(
        functools.partial(_ffn_kernel, mod_row=mod_row, n_side=len(side)),
        grid=(nb, ni, nf),
        in_specs=[
            pl.BlockSpec((None, tm, d), lambda b, i, f: (b, i, 0)),
            pl.BlockSpec(mod.shape, lambda b, i, f: (0, 0)),
            pl.BlockSpec((1, d), lambda b, i, f: (0, 0)),
            pl.BlockSpec((d, tf), lambda b, i, f: (0, f)),
            pl.BlockSpec((d, tf), lambda b, i, f: (0, f)),
            pl.BlockSpec((tf, d), lambda b, i, f: (f, 0)),
            *side_in,
        ],
        out_specs=[pl.BlockSpec((None, tm, d), lambda b, i, f: (b, i, 0)), *side_out],
        out_shape=[jax.ShapeDtypeStruct(x.shape, F32), *side_shapes],
        scratch_shapes=[pltpu.VMEM((tm, d), BF16), pltpu.VMEM((tm, d), F32)],
        compiler_params=_params(("arbitrary", "arbitrary", "arbitrary")),
        name="ffn_swiglu",
    )(x, mod, nw, w1, w3, w2, *side)


def _qkv_kernel(x_ref, mod_ref, nw_ref, *rest, mod_row, j0):
    nj = 3 - j0
    w_refs, (gq_ref, gk_ref, o_ref) = rest[:nj], rest[nj:]
    tm, d = x_ref.shape
    row = pl.program_id(0) if mod_row is None else mod_row
    sh, sc, _ = _mod_slices(mod_ref, row, d, True)
    sub = min(SUB_ROWS, tm)

    def norm(k, _):
        return (_norm_mod(x_ref[pl.ds(k * sub, sub), :], nw_ref[...], sh, sc).astype(BF16),)

    def project(jj):
        def stage(k, state):
            return state[0], _dot(state[0], w_refs[jj][...])
        return stage

    def finish(jj):
        part = jj + j0

        def stage(k, state):
            h, y = state
            rows = pl.ds(k * sub, sub)
            if part == 2:
                o_ref[jj, rows, :] = y.astype(BF16)
                return (h,)
            g = gq_ref[...] if part == 0 else gk_ref[...]
            for hh in range(d // NA_HEAD_DIM):
                cols = slice(hh * NA_HEAD_DIM, (hh + 1) * NA_HEAD_DIM)
                blk = y[:, cols]
                ms = jnp.mean(blk * blk, axis=-1, keepdims=True)
                out = (blk * lax.rsqrt(ms + EPS)) * g
                if part == 0:
                    out = out * (NA_HEAD_DIM ** -0.5)
                o_ref[jj, rows, cols] = out.astype(BF16)
            return (h,)
        return stage

    stages = [norm]
    for jj in range(nj):
        stages += [project(jj), finish(jj)]
    _skewed(stages, tm // sub)


def _qkv(x, mod, nw, w_qkv, g_q, g_k, *, mod_row, tm, j0):
    nb, s, d = x.shape
    tm = min(tm, s)
    nj = 3 - j0
    const = lambda *shape: pl.BlockSpec(shape, lambda b, i: (0,) * len(shape))
    return pl.pallas_call(
        functools.partial(_qkv_kernel, mod_row=mod_row, j0=j0),
        grid=(nb, s // tm),
        in_specs=[
            pl.BlockSpec((None, tm, d), lambda b, i: (b, i, 0)),
            const(*mod.shape), const(1, d),
            *[pl.BlockSpec((d, d), lambda b, i, part=jj + j0: (0, part)) for jj in range(nj)],
            const(1, NA_HEAD_DIM), const(1, NA_HEAD_DIM),
        ],
        out_specs=pl.BlockSpec((nj, None, tm, d), lambda b, i: (0, b, i, 0)),
        out_shape=jax.ShapeDtypeStruct((nj, nb, s, d), BF16),
        compiler_params=_params(("arbitrary", "arbitrary")),
        name="qkv_proj",
    )(x, mod, nw, *([w_qkv] * nj), g_q, g_k)


def _bias_table(rpb):
    qcol = np.arange(GRID_W)[:, None]
    kcol = np.arange(GRID_W)[None, :]
    cs = np.clip(qcol - NA_COLS // 2, 0, GRID_W - NA_COLS)
    inwin = (kcol >= cs) & (kcol < cs + NA_COLS)
    dc = np.clip(kcol - qcol + NA_COLS - 1, 0, 2 * NA_COLS - 2)
    tab = jnp.where(inwin[None, None], rpb.astype(F32)[:, :, dc], NEG_INF)
    return jnp.concatenate([tab[:, :-1], tab[:, 1:]], axis=-1)


def _attn_kernel(q_ref, k_ref, v_ref, kc_ref, vc_ref, tab_ref, o_ref, s_s, p_s, *, kh):
    d = q_ref.shape[-1]
    heads = d // NA_HEAD_DIM
    rows = k_ref.shape[0] // GRID_W
    n_loc = kh * GRID_W
    rows_per_step = q_ref.shape[0] // GRID_W
    units = []
    for rr in range(rows_per_step):
        r = pl.program_id(1) * rows_per_step + rr
        rs = jnp.clip(r - kh // 2, 0, rows - kh)
        dr0 = rs - r + (NA_ROWS - 1)
        win = pl.ds(pl.multiple_of(rs * GRID_W, GRID_W), n_loc)
        qrows = slice(rr * GRID_W, (rr + 1) * GRID_W)
        for hh in range(heads):
            units.append((rr * heads + hh, hh, qrows, slice(hh * NA_HEAD_DIM, (hh + 1) * NA_HEAD_DIM), dr0, win))
    for u, hh, qrows, cols, dr0, win in units:
        qh = q_ref[qrows, cols]
        bias = jnp.concatenate([tab_ref[hh, pl.ds(dr0 + 2 * j, 1)][0] for j in range(kh // 2)], axis=-1)
        s_s[u, :, :n_loc] = _dot_t(qh, k_ref[win, cols]) + bias
        s_s[u, :, n_loc:] = _dot_t(qh, kc_ref[:, cols])
    dens = []
    for u in range(len(units)):
        s = s_s[u]
        p = jnp.exp(s - jnp.max(s, axis=-1, keepdims=True))
        dens.append(jnp.sum(p, axis=-1, keepdims=True))
        p_s[u] = p.astype(BF16)
    for u, hh, qrows, cols, dr0, win in units:
        o = _dot(p_s[u, :, :n_loc], v_ref[win, cols]) + _dot(p_s[u, :, n_loc:], vc_ref[:, cols])
        o_ref[qrows, cols] = (o / dens[u]).astype(BF16)


def _attention(qkv, kvc, tab):
    _, nb, s, d = qkv.shape
    n_ctx = kvc.shape[2] // nb
    rows = s // GRID_W
    kh = min(NA_ROWS, rows)
    once = pl.Buffered(1)
    rps = 2 if rows % 2 == 0 else 1
    n_units = rps * (d // NA_HEAD_DIM)
    return pl.pallas_call(
        functools.partial(_attn_kernel, kh=kh),
        grid=(nb, rows // rps),
        in_specs=[
            pl.BlockSpec((None, None, rps * GRID_W, d), lambda b, r: (0, b, r, 0)),
            pl.BlockSpec((None, None, s, d), lambda b, r: (1, b, 0, 0), pipeline_mode=once),
            pl.BlockSpec((None, None, s, d), lambda b, r: (2, b, 0, 0), pipeline_mode=once),
            pl.BlockSpec((None, None, n_ctx, d), lambda b, r: (0, 0, b, 0)),
            pl.BlockSpec((None, None, n_ctx, d), lambda b, r: (1, 0, b, 0)),
            pl.BlockSpec(tab.shape, lambda b, r: (0, 0, 0, 0), pipeline_mode=once),
        ],
        out_specs=pl.BlockSpec((None, rps * GRID_W, d), lambda b, r: (b, r, 0)),
        out_shape=jax.ShapeDtypeStruct((nb, s, d), BF16),
        scratch_shapes=[pltpu.VMEM((n_units, GRID_W, kh * GRID_W + n_ctx), F32),
                        pltpu.VMEM((n_units, GRID_W, kh * GRID_W + n_ctx), BF16)],
        compiler_params=_params(("arbitrary", "arbitrary")),
        name="nbr_attention",
    )(qkv, qkv, qkv, kvc, kvc, tab)


def _proj_route_kernel(o_ref, x_ref, mod_ref, nw_ref, wo_ref, wrh_ref, wrl_ref, x2_ref, tok_ref, route_ref):
    tm, d = x_ref.shape
    row = pl.program_id(0)
    _, _, gate1 = _mod_slices(mod_ref, row, d, True)
    sh, sc, _ = _mod_slices(mod_ref, row, d, False)
    sub = min(SUB_ROWS, tm)
    lane = lax.broadcasted_iota(jnp.int32, (sub, LANE), 1)

    def project(k, _):
        rows = pl.ds(k * sub, sub)
        return _dot(o_ref[rows, :], wo_ref[...])

    def residual_norm(k, y):
        rows = pl.ds(k * sub, sub)
        x2 = x_ref[rows, :] + gate1 * y
        x2_ref[rows, :] = x2
        t = _norm_mod(x2, nw_ref[...], sh, sc)
        tok_ref[rows, :] = t
        return _split_bf16(t)

    def router(k, t_split):
        t_hi, t_lo = t_split
        return _dot(t_hi, wrh_ref[...]) + _dot(t_lo, wrh_ref[...]) + _dot(t_hi, wrl_ref[...])

    def top2(k, logits):
        logits = jnp.where(lane < MOE_EXPERTS, logits, -jnp.inf)
        m1 = jnp.max(logits, axis=-1, keepdims=True)
        i1 = jnp.min(jnp.where(logits == m1, lane, LANE), axis=-1, keepdims=True)
        rest = jnp.where(lane == i1, -jnp.inf, logits)
        m2 = jnp.max(rest, axis=-1, keepdims=True)
        i2 = jnp.min(jnp.where(rest == m2, lane, LANE), axis=-1, keepdims=True)
        e = jnp.exp(m2 - m1)
        den = 1.0 + e
        route_ref[pl.ds(k * sub, sub), :] = jnp.where(
            lane == 0, i1.astype(F32),
            jnp.where(lane == 1, i2.astype(F32),
                      jnp.where(lane == 2, 1.0 / den, jnp.where(lane == 3, e / den, 0.0))))

    _skewed((project, residual_norm, router, top2), tm // sub)


def _proj_route(o, x, mod, nw, w_o, wr_hi, wr_lo, *, tm):
    nb, s, d = x.shape
    const = lambda *shape: pl.BlockSpec(shape, lambda b, i: (0,) * len(shape))
    tile = lambda w: pl.BlockSpec((None, tm, w), lambda b, i: (b, i, 0))
    return pl.pallas_call(
        _proj_route_kernel,
        grid=(nb, s // tm),
        in_specs=[tile(d), tile(d), const(*mod.shape), const(1, d), const(d, d), const(d, LANE), const(d, LANE)],
        out_specs=[tile(d), tile(d), tile(LANE)],
        out_shape=[jax.ShapeDtypeStruct(x.shape, F32), jax.ShapeDtypeStruct(x.shape, F32),
                   jax.ShapeDtypeStruct((nb, s, LANE), F32)],
        compiler_params=_params(("arbitrary", "arbitrary")),
        name="attn_proj_route",
    )(o, x, mod, nw, w_o, wr_hi, wr_lo)


def _expert_kernel(be_ref, nu_ref, st_ref, stn_ref, tok_hbm, w1_ref, w3_ref, w2_ref, y_ref,
                   xf_s, xb_s, acc_s, sems, *, groups_per_step):
    del be_ref
    b = pl.program_id(0)
    f = pl.program_id(1)
    bm = xb_s.shape[0]
    n_groups = xf_s.shape[1]
    last = f == pl.num_programs(1) - 1
    n_used = nu_ref[0]
    used = b < n_used
    cur = b % 2

    def row_copy(tok, grp, u, buf):
        return pltpu.make_async_copy(tok_hbm.at[pl.ds(tok, 1)], xf_s.at[buf, grp, pl.ds(u, 1)], sems.at[buf])

    def request_group(idx_ref, grp, buf):
        for u in range(ROW_GROUP):
            row_copy(idx_ref[0, grp * ROW_GROUP + u], grp, u, buf).start(priority=u % 2)

    def wait_all(buf):
        def wait(i, carry):
            row_copy(0, 0, 0, buf).wait()
            return carry
        lax.fori_loop(0, n_groups * ROW_GROUP, wait, 0, unroll=8)

    def request(idx_ref, buf, g0, n):
        def body(i, carry):
            request_group(idx_ref, g0 + i, buf)
            return carry
        lax.fori_loop(0, n, body, 0)

    def up(xb):
        return (_silu(_dot(xb, w1_ref[...])) * _dot(xb, w3_ref[...])).astype(BF16)

    @pl.when(used)
    def _():
        @pl.when(jnp.logical_and(b == 0, f == 0))
        def _():
            request(st_ref, 0, 0, n_groups)

        @pl.when(jnp.logical_and(f < n_groups // groups_per_step, b + 1 < n_used))
        def _():
            request(stn_ref, 1 - cur, f * groups_per_step, groups_per_step)

        @pl.when(f == 0)
        def _():
            wait_all(cur)
            sub = min(SUB_ROWS, bm)
            sub_groups = sub // ROW_GROUP

            def cast(k, _):
                xb = xf_s[cur, pl.ds(k * sub_groups, sub_groups)].reshape(sub, xb_s.shape[1]).astype(BF16)
                xb_s[pl.ds(k * sub, sub), :] = xb
                return xb

            def down(k, hmid):
                acc_s[pl.ds(k * sub, sub), :] = _dot(hmid, w2_ref[...])

            _skewed((cast, lambda k, xb: up(xb), down), bm // sub)

        @pl.when(f > 0)
        def _():
            acc_s[...] += _dot(up(xb_s[...]), w2_ref[...])

        @pl.when(last)
        def _():
            y_ref[...] = acc_s[...]

    @pl.when(jnp.logical_and(jnp.logical_not(used), last))
    def _():
        y_ref[...] = jnp.zeros_like(y_ref)


def _experts(blk_e, n_used, slot_tok, toks, w1, w3, w2, *, bm, tf):
    n_blocks = slot_tok.shape[0]
    d = toks.shape[1]
    nf = w1.shape[2] // tf
    n_groups = bm // ROW_GROUP
    issue_steps = next(n for n in (8, 4, 2, 1) if n <= nf)
    groups_per_step = n_groups // issue_steps
    slot_tok = slot_tok.reshape(n_blocks, 1, bm)

    def wcol(b, f, be, nu):
        return (be[b], 0, jnp.where(b < nu[0], f, nf - 1))

    def wrow(b, f, be, nu):
        return (be[b], jnp.where(b < nu[0], f, nf - 1), 0)

    return pl.pallas_call(
        functools.partial(_expert_kernel, groups_per_step=groups_per_step),
        grid_spec=pltpu.PrefetchScalarGridSpec(
            num_scalar_prefetch=2,
            grid=(n_blocks, nf),
            in_specs=[
                pl.BlockSpec((None, 1, n_groups * ROW_GROUP), lambda b, f, be, nu: (b, 0, 0),
                             memory_space=pltpu.SMEM),
                pl.BlockSpec((None, 1, n_groups * ROW_GROUP),
                             lambda b, f, be, nu: (jnp.minimum(b + 1, n_blocks - 1), 0, 0),
                             memory_space=pltpu.SMEM),
                pl.BlockSpec(memory_space=pl.ANY),
                pl.BlockSpec((None, d, tf), wcol),
                pl.BlockSpec((None, d, tf), wcol),
                pl.BlockSpec((None, tf, d), wrow),
            ],
            out_specs=pl.BlockSpec((bm, d), lambda b, f, be, nu: (b, 0)),
            scratch_shapes=[pltpu.VMEM((2, n_groups, ROW_GROUP, d), F32), pltpu.VMEM((bm, d), BF16),
                            pltpu.VMEM((bm, d), F32),
                            pltpu.SemaphoreType.DMA((2,))],
        ),
        out_shape=jax.ShapeDtypeStruct((n_blocks * bm, d), F32),
        compiler_params=_params(("arbitrary", "arbitrary")),
        name="moe_experts",
    )(blk_e, n_used, slot_tok, slot_tok, toks, w1, w3, w2)


def _combine_kernel(slot_ref, slotn_ref, route_ref, x_ref, mod_ref, y_hbm, o_ref, y_s, sems):
    rows, d = x_ref.shape
    step = pl.program_id(0) * pl.num_programs(1) + pl.program_id(1)
    n_steps = pl.num_programs(0) * pl.num_programs(1)
    cur = step % 2

    def row_copy(s, buf, k, grp, u):
        return pltpu.make_async_copy(y_hbm.at[pl.ds(s, 1)], y_s.at[buf, k, grp, pl.ds(u, 1)], sems.at[buf])

    def request(idx_ref, buf):
        def body(grp, carry):
            for u in range(ROW_GROUP):
                for k in range(MOE_TOP_K):
                    row_copy(idx_ref[0, 0, MOE_TOP_K * (grp * ROW_GROUP + u) + k], buf, k, grp, u).start(priority=k)
            return carry
        lax.fori_loop(0, rows // ROW_GROUP, body, 0)

    @pl.when(step == 0)
    def _():
        request(slot_ref, 0)

    @pl.when(step + 1 < n_steps)
    def _():
        request(slotn_ref, 1 - cur)

    def wait(i, carry):
        row_copy(0, cur, 0, 0, 0).wait()
        return carry

    lax.fori_loop(0, rows * MOE_TOP_K, wait, 0, unroll=8)
    _, _, gate2 = _mod_slices(mod_ref, pl.program_id(0), d, False)
    route = route_ref[...]
    f = route[:, 2:3] * y_s[cur, 0].reshape(rows, d) + route[:, 3:4] * y_s[cur, 1].reshape(rows, d)
    o_ref[...] = x_ref[...] + gate2 * f


def _combine(slot, route, x, mod, y, *, rows):
    nb, s, d = x.shape
    steps = s // rows
    idx_spec = lambda ahead: pl.BlockSpec(
        (1, 1, MOE_TOP_K * rows), lambda b, i: (jnp.minimum(b * steps + i + ahead, nb * steps - 1), 0, 0),
        memory_space=pltpu.SMEM)
    slot = slot.reshape(nb * steps, 1, MOE_TOP_K * rows)
    return pl.pallas_call(
        _combine_kernel,
        grid=(nb, steps),
        in_specs=[
            idx_spec(0), idx_spec(1),
            pl.BlockSpec((None, rows, LANE), lambda b, i: (b, i, 0)),
            pl.BlockSpec((None, rows, d), lambda b, i: (b, i, 0)),
            pl.BlockSpec(mod.shape, lambda b, i: (0, 0)),
            pl.BlockSpec(memory_space=pl.ANY),
        ],
        out_specs=pl.BlockSpec((None, rows, d), lambda b, i: (b, i, 0)),
        out_shape=jax.ShapeDtypeStruct(x.shape, F32),
        scratch_shapes=[pltpu.VMEM((2, MOE_TOP_K, rows // ROW_GROUP, ROW_GROUP, d), F32),
                        pltpu.SemaphoreType.DMA((2,))],
        compiler_params=_params(("arbitrary", "arbitrary")),
        name="moe_combine",
    )(slot, slot, route, x, mod, y)


def _moe_plan(route, bm):
    n_tok = route.shape[0]
    e_flat = route[:, :MOE_TOP_K].astype(jnp.int32).reshape(-1)
    onehot = (e_flat[:, None] == jnp.arange(MOE_EXPERTS)[None, :]).astype(jnp.int32)
    csum = jnp.cumsum(onehot, axis=0)
    counts = csum[-1]
    rank = jnp.sum((csum - onehot) * onehot, axis=1)
    padded = (counts + bm - 1) // bm * bm
    pad_end = jnp.cumsum(padded)
    slot = (pad_end - padded)[e_flat] + rank
    n_blocks = (n_tok * MOE_TOP_K) // bm + MOE_EXPERTS
    blk_e = jnp.minimum(jnp.searchsorted(pad_end, jnp.arange(n_blocks) * bm, side="right"), MOE_EXPERTS - 1)
    slot = slot.astype(jnp.int32)
    slot_tok = jnp.zeros((n_blocks * bm,), jnp.int32).at[slot].set(
        jnp.arange(n_tok * MOE_TOP_K, dtype=jnp.int32) // MOE_TOP_K, unique_indices=True)
    return (slot, slot_tok.reshape(n_blocks, bm), blk_e.astype(jnp.int32),
            (pad_end[-1:] // bm).astype(jnp.int32))


def kernel(x, c, ctx, c_ctx, w_ada, b_ada, norm_w, mlp_w_in, mlp_g_v, mlp_w_s, mlp_b_s, mlp_w_out,
           ffn_w1, ffn_w3, ffn_w2, na_w_qkv, na_g_q, na_g_k, na_rpb, na_w_o,
           moe_w_router, moe_w1, moe_w3, moe_w2):
    nb, s, d = x.shape
    n_ctx = ctx.shape[1]
    assert nb + 1 <= MOD_ROWS and w_ada.shape[0] == 2
    inner = mlp_w_out.shape[1]
    assert inner // MLP_GROUPS == LANE and d % NA_HEAD_DIM == 0

    cc = jnp.zeros((MOD_ROWS, d), F32).at[:nb].set(c).at[nb].set(c_ctx)
    mod = _ada_mod(cc, w_ada, b_ada)
    ctx_flat = ctx.reshape(1, nb * n_ctx, d)

    nw0 = norm_w[0]
    w_in = mlp_w_in[0].astype(BF16)
    w_s = mlp_w_s[0].astype(BF16)
    w_out = mlp_w_out[0].astype(BF16)
    g_v = mlp_g_v[0][None]
    b_full = jnp.repeat(mlp_b_s[0].T, inner // MLP_GROUPS, axis=1)
    mix = functools.partial(_mixer, mod=mod[0], nw=nw0[0][None], w_in=w_in, g_v=g_v, w_s=w_s, b_full=b_full,
                            w_out=w_out, tm=256)
    n_exp, _, moe_dim = moe_w1.shape[1:]
    x1, e_w2, f_w1, f_w3, f_w2, w_qkv, w_o = mix(
        x, mod_row=None, side=(moe_w2[0].reshape(n_exp * moe_dim, d), ffn_w1[0], ffn_w3[0], ffn_w2[0],
                               na_w_qkv[0], na_w_o[0]))
    xc1, = mix(ctx_flat, mod_row=nb)
    ffn = functools.partial(_ffn, mod=mod[0], nw=nw0[1][None], w1=f_w1, w3=f_w3, w2=f_w2, tm=512, tf=512)
    x1, e_w1, e_w3 = ffn(x1, mod_row=None, side=(moe_w1[0].reshape(n_exp * d, moe_dim),
                                                  moe_w3[0].reshape(n_exp * d, moe_dim)))
    xc1, = ffn(xc1, mod_row=nb)

    nw1 = norm_w[1]
    qkv_fn = functools.partial(_qkv, mod=mod[1], nw=nw1[0][None], w_qkv=w_qkv, g_q=na_g_q[0][None],
                               g_k=na_g_k[0][None], tm=512)
    qkv = qkv_fn(x1, mod_row=None, j0=0)
    kvc = qkv_fn(xc1, mod_row=nb, j0=1)
    o = _attention(qkv, kvc, _bias_table(na_rpb[0]))

    wr = jnp.zeros((d, LANE), F32).at[:, :MOE_EXPERTS].set(moe_w_router[0])
    wr_hi, wr_lo = _split_bf16(wr)
    x2, toks, route = _proj_route(o, x1, mod[1], nw1[1][None], w_o, wr_hi, wr_lo, tm=512)

    bm = 512
    n_tok = nb * s
    route_flat = route.reshape(n_tok, LANE)
    slot, slot_tok, blk_e, n_used = _moe_plan(route_flat, bm)
    ys = _experts(blk_e, n_used, slot_tok, toks.reshape(n_tok, d), e_w1.reshape(n_exp, d, moe_dim),
                  e_w3.reshape(n_exp, d, moe_dim), e_w2.reshape(n_exp, moe_dim, d), bm=bm, tf=min(1024, moe_dim))
    return _combine(slot, route, x2, mod[1], ys, rows=256)
```
